```python
import jax
import jax.numpy as jnp
from jax import lax
import numpy as np

D_MODEL = 1024
BATCH = 16
SEQ = 4096
DEPTH = 4

HEAD_DIM = 64
N_HEADS_A = 8
N_HEADS_B = 8
WIDTH_A = N_HEADS_A * HEAD_DIM
WIDTH_B = N_HEADS_B * HEAD_DIM
DILATED_CONFIGS = ((128, 1), (512, 4), (2048, 16))
SEG = 128
Q_BLOCK = 128
RET_HEADS = 4
RET_QK_DIM = D_MODEL // RET_HEADS
RET_V_DIM = 2 * D_MODEL // RET_HEADS
RET_CHUNK = 128
MLP_HIDDEN = 4 * D_MODEL
RMS_EPS = 1e-6
GN_EPS = 1e-6
EVEN_IN = 3 * WIDTH_A + 3 * WIDTH_B + N_HEADS_B
ODD_IN = 6 * D_MODEL
N_EVEN = (DEPTH + 1) // 2
N_ODD = DEPTH // 2

kernel_name = "hybrid_dilated_fox_retention_trunk"


def _rms_norm(x, g):
    xf = x.astype(jnp.float32)
    y = xf * lax.rsqrt(jnp.mean(xf * xf, axis=-1, keepdims=True) + RMS_EPS)
    return (y * g.astype(jnp.float32)).astype(x.dtype)


def _split_heads(t, n):
    b, s, _ = t.shape
    return t.reshape(b, s, n, -1).transpose(0, 2, 1, 3)


def _merge_heads(t):
    b, h, s, e = t.shape
    return t.transpose(0, 2, 1, 3).reshape(b, s, h * e)


def _alibi_slopes(n):
    return 2.0 ** (-8.0 * (jnp.arange(n, dtype=jnp.float32) + 1.0) / n)


def _retention_gammas(n):
    return 1.0 - 2.0 ** (-5.0 - jnp.arange(n, dtype=jnp.float32))


def _to_streams(t, dilation, padded_len):
    b, h, s, e = t.shape
    t = jnp.pad(t, ((0, 0), (0, 0), (0, padded_len - s), (0, 0)))
    t = t.reshape(b, h, padded_len // dilation, dilation, e)
    t = jnp.swapaxes(t, 2, 3)
    return t.reshape(b, h, dilation, padded_len // (dilation * SEG), SEG, e)


def _from_streams(t, s):
    b, h, d, n, _, e = t.shape
    t = t.reshape(b, h, d, n * SEG, e)
    t = jnp.swapaxes(t, 2, 3).reshape(b, h, d * n * SEG, e)
    return t[:, :, :s]


def _dilated_branch(q, k, v, slopes, window, dilation):
    b, h, s, dh = q.shape
    n_steps = window // dilation
    span = dilation * SEG
    padded = -(-s // span) * span
    qs = _to_streams(q, dilation, padded)
    ks = _to_streams(k, dilation, padded)
    vs = _to_streams(v, dilation, padded)
    n_blk = qs.shape[3]
    prev_pad = ((0, 0), (0, 0), (0, 0), (1, 0), (0, 0), (0, 0))
    kk = jnp.concatenate([jnp.pad(ks, prev_pad)[:, :, :, :-1], ks], axis=4)
    vv = jnp.concatenate([jnp.pad(vs, prev_pad)[:, :, :, :-1], vs], axis=4)
    scores = jnp.einsum("bhrnqe,bhrnke->bhrnqk", qs, kk,
                        preferred_element_type=jnp.float32) * (dh ** -0.5)
    qi = jnp.arange(SEG)[:, None]
    ki = jnp.arange(2 * SEG)[None, :]
    steps = qi + SEG - ki
    band = (steps >= 0) & (steps <= n_steps)
    has_prev = (jnp.arange(n_blk) > 0)[:, None, None] | (ki >= SEG)[None]
    valid = band[None] & has_prev
    dist = (steps * dilation).astype(jnp.float32)
    logits = scores - slopes[None, :, None, None, None, None] * dist
    logits = jnp.where(valid, logits, -jnp.inf)
    m = jnp.max(logits, axis=-1, keepdims=True)
    p = jnp.exp(logits - m)
    den = jnp.sum(p, axis=-1, keepdims=True)
    o = jnp.einsum("bhrnqk,bhrnke->bhrnqe", p, vv.astype(jnp.float32)) / den
    lse = m + jnp.log(den)
    return _from_streams(o, s), _from_streams(lse, s)[..., 0]


def _dilated_attention(q, k, v, slopes):
    outs, lses = [], []
    for window, dilation in DILATED_CONFIGS:
        o, l = _dilated_branch(q, k, v, slopes, window, dilation)
        outs.append(o)
        lses.append(l)
    w = jax.nn.softmax(jnp.stack(lses, axis=0), axis=0)
    return jnp.einsum("gbhs,gbhse->bhse", w, jnp.stack(outs, axis=0))


def _forgetting_attention(q, k, v, log_f):
    b, h, s, dh = q.shape
    c = jnp.cumsum(log_f, axis=-1)
    nb = s // Q_BLOCK
    qb = q.reshape(b, h, nb, Q_BLOCK, dh).transpose(2, 0, 1, 3, 4)
    cb = c.reshape(b, h, nb, Q_BLOCK).transpose(2, 0, 1, 3)
    kpos = jnp.arange(s)
    vf = v.astype(jnp.float32)

    def block(args):
        i, q_i, c_i = args
        sc = jnp.einsum("bhqe,bhke->bhqk", q_i, k,
                        preferred_element_type=jnp.float32) * (dh ** -0.5)
        sc = sc + c_i[..., None] - c[:, :, None, :]
        qpos = i * Q_BLOCK + jnp.arange(Q_BLOCK)
        sc = jnp.where(kpos[None, :] <= qpos[:, None], sc, -jnp.inf)
        p = jax.nn.softmax(sc, axis=-1)
        return jnp.einsum("bhqk,bhke->bhqe", p, vf)

    o = lax.map(block, (jnp.arange(nb), qb, cb))
    return o.transpose(1, 2, 0, 3, 4).reshape(b, h, s, dh)


def _retention(q, k, v, gammas):
    b, h, s, dk = q.shape
    dv = v.shape[-1]
    C = RET_CHUNK
    n = s // C
    q = q.astype(jnp.float32)
    k = k.astype(jnp.float32) * (dk ** -0.5)
    v = v.astype(jnp.float32)
    log_g = jnp.log(gammas)
    idx = jnp.arange(C, dtype=jnp.float32)
    rel = idx[:, None] - idx[None, :]
    decay_in = jnp.where(rel >= 0, jnp.exp(log_g[:, None, None] * jnp.maximum(rel, 0.0)), 0.0)
    q_dec = jnp.exp(log_g[:, None] * (idx + 1.0))[None, :, :, None]
    k_dec = jnp.exp(log_g[:, None] * (C - 1.0 - idx))[None, :, :, None]
    chunk_dec = jnp.exp(log_g * C)[None, :, None, None]

    def chunks(t):
        return t.reshape(b, h, n, C, t.shape[-1]).transpose(2, 0, 1, 3, 4)

    def step(state, inp):
        q_i, k_i, v_i = inp
        inner = jnp.einsum("bhqd,bhkd->bhqk", q_i, k_i) * decay_in
        o = (jnp.einsum("bhqk,bhkv->bhqv", inner, v_i)
             + jnp.einsum("bhqd,bhdv->bhqv", q_i, state) * q_dec)
        state = chunk_dec * state + jnp.einsum("bhkd,bhkv->bhdv", k_i * k_dec, v_i)
        return state, o

    state0 = jnp.zeros((b, h, dk, dv), jnp.float32)
    _, o = lax.scan(step, state0, (chunks(q), chunks(k), chunks(v)))
    return o.transpose(1, 2, 0, 3, 4).reshape(b, h, s, dv)


def _even_mixer(h, w_in, b_f, w_out, slopes):
    proj = h @ w_in
    cuts = [WIDTH_A, 2 * WIDTH_A, 3 * WIDTH_A,
            3 * WIDTH_A + WIDTH_B, 3 * WIDTH_A + 2 * WIDTH_B, 3 * WIDTH_A + 3 * WIDTH_B]
    qa, ka, va, qb, kb, vb, f_logit = jnp.split(proj, cuts, axis=-1)
    oa = _dilated_attention(_split_heads(qa, N_HEADS_A), _split_heads(ka, N_HEADS_A),
                            _split_heads(va, N_HEADS_A), slopes)
    log_f = jax.nn.log_sigmoid(f_logit.astype(jnp.float32) + b_f.astype(jnp.float32))
    ob = _forgetting_attention(_split_heads(qb, N_HEADS_B), _split_heads(kb, N_HEADS_B),
                               _split_heads(vb, N_HEADS_B), jnp.swapaxes(log_f, 1, 2))
    o = jnp.concatenate([_merge_heads(oa), _merge_heads(ob)], axis=-1).astype(h.dtype)
    return o @ w_out


def _odd_mixer(h, w_in, w_out, gammas):
    proj = h @ w_in
    q, k, v, g = jnp.split(proj, [D_MODEL, 2 * D_MODEL, 4 * D_MODEL], axis=-1)
    o = _retention(_split_heads(q, RET_HEADS), _split_heads(k, RET_HEADS),
                   _split_heads(v, RET_HEADS), gammas)
    mu = jnp.mean(o, axis=-1, keepdims=True)
    var = jnp.mean(jnp.square(o - mu), axis=-1, keepdims=True)
    o = _merge_heads((o - mu) * lax.rsqrt(var + GN_EPS))
    y = (jax.nn.silu(g.astype(jnp.float32)) * o).astype(h.dtype)
    return y @ w_out


def _squared_relu_mlp(h, w1, w2):
    return jnp.square(jax.nn.relu(h @ w1)) @ w2


def setup_inputs(seed: int = 0) -> dict:
    key = jax.random.key(seed)
    ks = jax.random.split(key, 12)
    f32 = jnp.float32
    out_scale = (2.0 * DEPTH) ** -0.5

    def dense(k, shape, fan_in, scale=1.0):
        return jax.random.normal(k, shape, f32) * (scale * fan_in ** -0.5)

    return {
        "x": jax.random.normal(ks[0], (BATCH, SEQ, D_MODEL), f32),
        "norm_mix_g": 1.0 + 0.02 * jax.random.normal(ks[1], (DEPTH, D_MODEL), f32),
        "norm_ffn_g": 1.0 + 0.02 * jax.random.normal(ks[2], (DEPTH, D_MODEL), f32),
        "even_w_in": dense(ks[3], (N_EVEN, D_MODEL, EVEN_IN), D_MODEL),
        "even_b_f": 2.0 + 0.5 * jax.random.normal(ks[4], (N_EVEN, N_HEADS_B), f32),
        "even_w_out": dense(ks[5], (N_EVEN, WIDTH_A + WIDTH_B, D_MODEL), WIDTH_A + WIDTH_B, out_scale),
        "odd_w_in": dense(ks[6], (N_ODD, D_MODEL, ODD_IN), D_MODEL),
        "odd_w_out": dense(ks[7], (N_ODD, 2 * D_MODEL, D_MODEL), 2 * D_MODEL, out_scale),
        "ffn_w_in": dense(ks[8], (DEPTH, D_MODEL, MLP_HIDDEN), D_MODEL),
        "ffn_w_out": dense(ks[9], (DEPTH, MLP_HIDDEN, D_MODEL), MLP_HIDDEN, out_scale),
        "final_g": 1.0 + 0.02 * jax.random.normal(ks[10], (D_MODEL,), f32),
    }


def reference(x, norm_mix_g, norm_ffn_g, even_w_in, even_b_f, even_w_out,
              odd_w_in, odd_w_out, ffn_w_in, ffn_w_out, final_g):
    slopes = _alibi_slopes(N_HEADS_A)
    gammas = _retention_gammas(RET_HEADS)
    for layer in range(DEPTH):
        h = _rms_norm(x, norm_mix_g[layer])
        if layer % 2 == 0:
            i = layer // 2
            x = x + _even_mixer(h, even_w_in[i], even_b_f[i], even_w_out[i], slopes)
        else:
            i = layer // 2
            x = x + _odd_mixer(h, odd_w_in[i], odd_w_out[i], gammas)
        h = _rms_norm(x, norm_ffn_g[layer])
        x = x + _squared_relu_mlp(h, ffn_w_in[layer], ffn_w_out[layer])
    return _rms_norm(x, final_g)
```

```python
import functools

import jax
import jax.numpy as jnp
from jax import lax
from jax.experimental import pallas as pl
from jax.experimental.pallas import tpu as pltpu

F32 = jnp.float32
BF16 = jnp.bfloat16

HEAD_DIM = 64
N_HEADS_A = 8
N_HEADS_B = 8
WIDTH_A = N_HEADS_A * HEAD_DIM
WIDTH_B = N_HEADS_B * HEAD_DIM
DILATIONS = (1, 4, 16)
SEG = 128
RET_HEADS = 4
RET_CHUNK = 128
RMS_EPS = 1e-6
GN_EPS = 1e-6
LANES = 128
NEG = -1e30
V7X_VMEM_LIMIT = 56 * 1024 * 1024

_NT = (((1,), (1,)), ((), ()))
_TN = (((0,), (0,)), ((), ()))


def _params(*sem):
    return pltpu.CompilerParams(dimension_semantics=sem, vmem_limit_bytes=V7X_VMEM_LIMIT)


def _resident(shape):
    nd = len(shape)
    return pl.BlockSpec(shape, lambda *_: (0,) * nd, pipeline_mode=pl.Buffered(1))


def _rms(x, g):
    ms = jnp.mean(x * x, axis=-1, keepdims=True)
    return x * lax.rsqrt(ms + RMS_EPS) * g


def _norm_proj_kernel(x_ref, g_ref, w_ref, *rest, tn, with_gate):
    if with_gate:
        wf_ref, o_ref, f_ref = rest
    else:
        (o_ref,) = rest
    h = _rms(x_ref[...], g_ref[...]).astype(BF16)
    for c in range(w_ref.shape[1] // tn):
        sl = slice(c * tn, (c + 1) * tn)
        o_ref[:, sl] = jnp.dot(h, w_ref[:, sl], preferred_element_type=F32).astype(o_ref.dtype)
    if with_gate:
        f_ref[...] = lax.dot_general(wf_ref[...], h, _NT, preferred_element_type=F32)


def _norm_proj(x2, g, w, wf_t=None, *, tm=512, tn=512):
    t, d = x2.shape
    n = w.shape[1]
    with_gate = wf_t is not None
    in_specs = [pl.BlockSpec((tm, d), lambda i: (i, 0)), _resident((1, d)), _resident((d, n))]
    out_specs = [pl.BlockSpec((tm, n), lambda i: (i, 0))]
    out_shape = [jax.ShapeDtypeStruct((t, n), BF16)]
    args = [x2, g.reshape(1, d), w]
    if with_gate:
        nh = wf_t.shape[0]
        in_specs.append(_resident((nh, d)))
        out_specs.append(pl.BlockSpec((nh, tm), lambda i: (0, i)))
        out_shape.append(jax.ShapeDtypeStruct((nh, t), F32))
        args.append(wf_t)
    res = pl.pallas_call(
        functools.partial(_norm_proj_kernel, tn=tn, with_gate=with_gate),
        grid=(t // tm,), in_specs=in_specs, out_specs=out_specs, out_shape=out_shape,
        compiler_params=_params("parallel"), name="norm_proj_gate" if with_gate else "norm_proj",
    )(*args)
    return res if with_gate else res[0]


def _gate_cumsum_kernel(f_ref, b_ref, c_ref):
    nh, s = f_ref.shape
    row = lax.broadcasted_iota(jnp.int32, (LANES, LANES), 0)
    col = lax.broadcasted_iota(jnp.int32, (LANES, LANES), 1)
    upper = (row <= col).astype(BF16)
    carry = jnp.zeros((nh, 1), F32)
    for n in range(s // LANES):
        sl = slice(n * LANES, (n + 1) * LANES)
        z = f_ref[:, sl] + b_ref[...]
        x = jnp.minimum(z, 0.0) - jnp.log1p(jnp.exp(-jnp.abs(z)))
        hi = x.astype(BF16)
        r1 = x - hi.astype(F32)
        mid = r1.astype(BF16)
        lo = (r1 - mid.astype(F32)).astype(BF16)
        c = (jnp.dot(hi, upper, preferred_element_type=F32)
             + jnp.dot(mid, upper, preferred_element_type=F32)
             + jnp.dot(lo, upper, preferred_element_type=F32)) + carry
        c_ref[:, sl] = c
        carry = c[:, LANES - 1:LANES]


def _gate_cumsum(f_t, b_f, *, batch):
    nh, t = f_t.shape
    s = t // batch
    return pl.pallas_call(
        _gate_cumsum_kernel, grid=(batch,),
        in_specs=[pl.BlockSpec((nh, s), lambda b: (0, b)), _resident((nh, 1))],
        out_specs=pl.BlockSpec((nh, s), lambda b: (0, b)),
        out_shape=jax.ShapeDtypeStruct((nh, t), F32),
        compiler_params=_params("parallel"), name="gate_cumsum",
    )(f_t, b_f.reshape(nh, 1).astype(F32))


def _dilated_kernel(slopes_ref, q_ref, k_ref, v_ref, o_ref,
                    qf, kf, vf, qs, ks, vs, ob, lb, bias):
    pair = pl.program_id(1)
    s = q_ref.shape[0]
    nblk = s // SEG
    lane_head = lax.broadcasted_iota(jnp.int32, (SEG, LANES), 1) // HEAD_DIM
    qi = lax.broadcasted_iota(jnp.int32, (SEG, 2 * SEG), 0)
    ki = lax.broadcasted_iota(jnp.int32, (SEG, 2 * SEG), 1)
    steps = qi + SEG - ki
    band = (steps >= 0) & (steps <= SEG)
    stepsf = steps.astype(F32)

    qf[...] = q_ref[...].astype(F32)
    kf[...] = k_ref[...].astype(F32)
    vf[...] = v_ref[...].astype(F32)
    ks[0:SEG, :] = jnp.zeros((SEG, LANES), BF16)
    vs[0:SEG, :] = jnp.zeros((SEG, LANES), BF16)

    for g, d in enumerate(DILATIONS):
        per_stream = nblk // d

        def stage(t, _, d=d, per_stream=per_stream):
            r = t // per_stream
            n = t - r * per_stream
            src = n * (SEG * d) + r
            dst = pl.multiple_of(t * SEG, SEG)
            rows = pl.ds(src, SEG, stride=d) if d > 1 else pl.ds(pl.multiple_of(src, SEG), SEG)
            qs[pl.ds(dst, SEG), :] = qf[rows, :].astype(BF16)
            ks[pl.ds(dst + SEG, SEG), :] = kf[rows, :].astype(BF16)
            vs[pl.ds(dst + SEG, SEG), :] = vf[rows, :].astype(BF16)
            return 0

        lax.fori_loop(0, nblk, stage, 0)

        for hh in range(2):
            slope = slopes_ref[2 * pair + hh] * float(d)
            b = jnp.where(band, -slope * stepsf, NEG)
            bias[2 * hh] = b
            bias[2 * hh + 1] = jnp.where(ki >= SEG, b, NEG)

        def tile(t, _, d=d, per_stream=per_stream, g=g):
            r = t // per_stream
            n = t - r * per_stream
            first = jnp.where(n == 0, 1, 0)
            base = pl.multiple_of(t * SEG, SEG)
            qt = qs[pl.ds(base, SEG), :]
            kk = ks[pl.ds(base, 2 * SEG), :]
            vv = vs[pl.ds(base, 2 * SEG), :]
            outs, lses = [], []
            for hh in range(2):
                qm = jnp.where(lane_head == hh, qt, jnp.zeros_like(qt))
                sc = lax.dot_general(qm, kk, _NT, preferred_element_type=F32)
                sc = sc + bias[2 * hh + first]
                m = jnp.max(sc, axis=-1, keepdims=True)
                p = jnp.exp(sc - m)
                den = jnp.sum(p, axis=-1, keepdims=True)
                acc = jnp.dot(p.astype(BF16), vv, preferred_element_type=F32)
                outs.append(acc / den)
                lses.append(m + jnp.log(den))
            o_t = jnp.where(lane_head == 0, outs[0], outs[1])
            l_t = jnp.where(lane_head == 0, lses[0], lses[1])
            dst = n * (SEG * d) + r
            rows = pl.ds(dst, SEG, stride=d) if d > 1 else pl.ds(pl.multiple_of(dst, SEG), SEG)
            ob[g, rows, :] = o_t
            lb[g, rows, :] = l_t
            return 0

        lax.fori_loop(0, nblk, tile, 0)

    def mix(n, _):
        rows = pl.ds(pl.multiple_of(n * SEG, SEG), SEG)
        l0, l1, l2 = lb[0, rows, :], lb[1, rows, :], lb[2, rows, :]
        m = jnp.maximum(jnp.maximum(l0, l1), l2)
        w0, w1, w2 = jnp.exp(l0 - m), jnp.exp(l1 - m), jnp.exp(l2 - m)
        num = w0 * ob[0, rows, :] + w1 * ob[1, rows, :] + w2 * ob[2, rows, :]
        o_ref[rows, :] = (num / (w0 + w1 + w2)).astype(o_ref.dtype)
        return 0

    lax.fori_loop(0, nblk, mix, 0)


def _dilated(qkv, slopes, *, batch):
    t, _ = qkv.shape
    s = t // batch
    assert s % (SEG * max(DILATIONS)) == 0
    pairs = WIDTH_A // LANES
    col = lambda off: pl.BlockSpec((s, LANES), lambda b, p: (b, off + p))
    return pl.pallas_call(
        _dilated_kernel, grid=(batch, pairs),
        in_specs=[pl.BlockSpec(memory_space=pltpu.SMEM), col(0), col(pairs), col(2 * pairs)],
        out_specs=pl.BlockSpec((s, LANES), lambda b, p: (b, p)),
        out_shape=jax.ShapeDtypeStruct((t, WIDTH_A), BF16),
        scratch_shapes=[pltpu.VMEM((s, LANES), F32)] * 3
        + [pltpu.VMEM((s, LANES), BF16), pltpu.VMEM((s + SEG, LANES), BF16), pltpu.VMEM((s + SEG, LANES), BF16)]
        + [pltpu.VMEM((len(DILATIONS), s, LANES), F32)] * 2
        + [pltpu.VMEM((4, SEG, 2 * SEG), F32)],
        compiler_params=_params("parallel", "parallel"), name="dilated",
    )(slopes, qkv, qkv, qkv)


def _fox_kernel(c_ref, q_ref, k_ref, v_ref, o_ref, *, tq):
    pair = pl.program_id(1)
    i = pl.program_id(2)
    q = q_ref[...]
    lane_head = lax.broadcasted_iota(jnp.int32, (tq, LANES), 1) // HEAD_DIM
    causal = (lax.broadcasted_iota(jnp.int32, (tq, tq), 1) <= lax.broadcasted_iota(jnp.int32, (tq, tq), 0))
    outs = []
    for hh in range(2):
        qm = jnp.where(lane_head == hh, q, jnp.zeros_like(q))
        crow = 2 * pair + hh

        def block(j, carry, masked, qm=qm, crow=crow):
            m, l, acc = carry
            cols = pl.ds(pl.multiple_of(j * tq, tq), tq)
            sc = lax.dot_general(qm, k_ref[cols, :], _NT, preferred_element_type=F32)
            sc = sc - c_ref[pl.ds(crow, 1), cols]
            if masked:
                sc = jnp.where(causal, sc, NEG)
            m_new = jnp.maximum(m, jnp.max(sc, axis=-1, keepdims=True))
            alpha = jnp.exp(m - m_new)
            p = jnp.exp(sc - m_new)
            l = alpha * l + jnp.sum(p, axis=-1, keepdims=True)
            acc = alpha * acc + jnp.dot(p.astype(BF16), v_ref[cols, :], preferred_element_type=F32)
            return m_new, l, acc

        init = (jnp.full((tq, 1), NEG, F32), jnp.zeros((tq, 1), F32), jnp.zeros((tq, LANES), F32))
        carry = lax.fori_loop(0, i, functools.partial(block, masked=False), init)
        _, l, acc = block(i, carry, True)
        outs.append(acc / l)
    o_ref[...] = jnp.where(lane_head == 0, outs[0], outs[1]).astype(o_ref.dtype)


def _fox(qkv, c, *, batch, tq=512):
    t, _ = qkv.shape
    s = t // batch
    nq = s // tq
    pairs = WIDTH_B // LANES
    base = 3 * WIDTH_A // LANES
    kv = lambda off: pl.BlockSpec((s, LANES), lambda b, p, i: (b, base + off + p))
    return pl.pallas_call(
        functools.partial(_fox_kernel, tq=tq), grid=(batch, pairs, nq),
        in_specs=[pl.BlockSpec((N_HEADS_B, s), lambda b, p, i: (0, b)),
                  pl.BlockSpec((tq, LANES), lambda b, p, i: (b * nq + i, base + p)),
                  kv(pairs), kv(2 * pairs)],
        out_specs=pl.BlockSpec((tq, LANES), lambda b, p, i: (b * nq + i, p)),
        out_shape=jax.ShapeDtypeStruct((t, WIDTH_B), BF16),
        compiler_params=_params("parallel", "parallel", "arbitrary"), name="fox",
    )(c, qkv, qkv, qkv)


def _retention_kernel(cdec_ref, q_ref, k_ref, v_ref, g_ref, din_ref, qdec_ref, kdec_ref, y_ref, state):
    head = pl.program_id(1)

    @pl.when(pl.program_id(2) == 0)
    def _():
        state[...] = jnp.zeros_like(state)

    chunk_dec = cdec_ref[head]
    c = RET_CHUNK
    for n in range(q_ref.shape[0] // c):
        rows = slice(n * c, (n + 1) * c)
        q = q_ref[rows, :]
        k = k_ref[rows, :]
        v = v_ref[rows, :]
        st = state[...]
        inner = lax.dot_general(q, k, _NT, preferred_element_type=F32) * din_ref[...]
        o = (jnp.dot(inner.astype(BF16), v, preferred_element_type=F32)
             + jnp.dot(q, st.astype(BF16), preferred_element_type=F32) * qdec_ref[...])
        kd = (k.astype(F32) * kdec_ref[...]).astype(BF16)
        state[...] = chunk_dec * st + lax.dot_general(kd, v, _TN, preferred_element_type=F32)
        mu = jnp.mean(o, axis=-1, keepdims=True)
        cen = o - mu
        var = jnp.mean(cen * cen, axis=-1, keepdims=True)
        gate = g_ref[rows, :].astype(F32)
        y_ref[rows, :] = (gate * jax.nn.sigmoid(gate) * (cen * lax.rsqrt(var + GN_EPS))).astype(y_ref.dtype)


def _retention(proj, *, batch, d_model, tb=1024):
    t, _ = proj.shape
    s = t // batch
    nb = s // tb
    dk = d_model // RET_HEADS
    dv = 2 * d_model // RET_HEADS
    c = RET_CHUNK
    log_g = jnp.log(1.0 - 2.0 ** (-5.0 - jnp.arange(RET_HEADS, dtype=F32)))
    idx = jnp.arange(c, dtype=F32)
    rel = idx[:, None] - idx[None, :]
    decay_in = jnp.where(rel >= 0, jnp.exp(log_g[:, None, None] * jnp.maximum(rel, 0.0)), 0.0)
    q_dec = jnp.exp(log_g[:, None] * (idx + 1.0))[:, :, None]
    k_dec = jnp.exp(log_g[:, None] * (c - 1.0 - idx))[:, :, None]
    chunk_dec = jnp.exp(log_g * c)
    per_head = lambda shape: pl.BlockSpec((None,) + shape, lambda b, h, i: (h, 0, 0))
    tok = lambda width, off: pl.BlockSpec((tb, width), lambda b, h, i: (b * nb + i, off + h))
    return pl.pallas_call(
        _retention_kernel, grid=(batch, RET_HEADS, nb),
        in_specs=[pl.BlockSpec(memory_space=pltpu.SMEM),
                  tok(dk, 0), tok(dk, d_model // dk), tok(dv, 2 * d_model // dv), tok(dv, 4 * d_model // dv),
                  per_head((c, c)), per_head((c, 1)), per_head((c, 1))],
        out_specs=tok(dv, 0),
        out_shape=jax.ShapeDtypeStruct((t, 2 * d_model), BF16),
        scratch_shapes=[pltpu.VMEM((dk, dv), F32)],
        compiler_params=_params("parallel", "parallel", "arbitrary"), name="retention",
    )(chunk_dec, proj, proj, proj, proj, decay_in, q_dec, k_dec)


def _post_mixer_kernel(*refs, n_in, th, final):
    x_ref = refs[0]
    ys = refs[1:1 + n_in]
    wo_ref, g_ref, w1_ref, w2_ref = refs[1 + n_in:5 + n_in]
    gf_ref = refs[5 + n_in] if final else None
    o_ref = refs[-1]
    y = jnp.concatenate([y_ref[...] for y_ref in ys], axis=1)
    x = x_ref[...] + jnp.dot(y, wo_ref[...], preferred_element_type=F32)
    h = _rms(x, g_ref[...]).astype(BF16)
    for c in range(w1_ref.shape[1] // th):
        sl = slice(c * th, (c + 1) * th)
        a = jnp.maximum(jnp.dot(h, w1_ref[:, sl], preferred_element_type=F32), 0.0)
        x = x + jnp.dot((a * a).astype(BF16), w2_ref[sl, :], preferred_element_type=F32)
    o_ref[...] = _rms(x, gf_ref[...]) if final else x


def _post_mixer(x2, ys, w_out, g, w1, w2, final_g=None, *, tm=512, th=512):
    t, d = x2.shape
    final = final_g is not None
    row = lambda width: pl.BlockSpec((tm, width), lambda i: (i, 0))
    in_specs = ([row(d)] + [row(y.shape[1]) for y in ys]
                + [_resident(w_out.shape), _resident((1, d)), _resident(w1.shape), _resident(w2.shape)])
    args = [x2, *ys, w_out, g.reshape(1, d), w1, w2]
    if final:
        in_specs.append(_resident((1, d)))
        args.append(final_g.reshape(1, d))
    return pl.pallas_call(
        functools.partial(_post_mixer_kernel, n_in=len(ys), th=th, final=final),
        grid=(t // tm,), in_specs=in_specs, out_specs=row(d),
        out_shape=jax.ShapeDtypeStruct((t, d), F32),
        compiler_params=_params("parallel"), name="post_mixer_final" if final else "post_mixer",
    )(*args)


def kernel(x, norm_mix_g, norm_ffn_g, even_w_in, even_b_f, even_w_out, odd_w_in, odd_w_out,
           ffn_w_in, ffn_w_out, final_g):
    batch, seq, d = x.shape
    depth = norm_mix_g.shape[0]
    x2 = x.reshape(batch * seq, d)
    slopes = 2.0 ** (-8.0 * (jnp.arange(N_HEADS_A, dtype=F32) + 1.0) / N_HEADS_A)
    n_qkv = 3 * WIDTH_A + 3 * WIDTH_B
    even_scale = jnp.ones((n_qkv,), F32).at[:WIDTH_A].set(HEAD_DIM ** -0.5)
    even_scale = even_scale.at[3 * WIDTH_A:3 * WIDTH_A + WIDTH_B].set(HEAD_DIM ** -0.5)
    odd_scale = jnp.ones((odd_w_in.shape[-1],), F32).at[d:2 * d].set((d // RET_HEADS) ** -0.5)
    for layer in range(depth):
        i = layer // 2
        if layer % 2 == 0:
            w = (even_w_in[i, :, :n_qkv] * even_scale).astype(BF16)
            wf_t = even_w_in[i, :, n_qkv:].T.astype(BF16)
            qkv, f_t = _norm_proj(x2, norm_mix_g[layer], w, wf_t)
            c = _gate_cumsum(f_t, even_b_f[i], batch=batch)
            oa = _dilated(qkv, slopes, batch=batch)
            ob = _fox(qkv, c, batch=batch)
            ys, w_out = [oa, ob], even_w_out[i].astype(BF16)
        else:
            w = (odd_w_in[i] * odd_scale).astype(BF16)
            proj = _norm_proj(x2, norm_mix_g[layer], w)
            ys, w_out = [_retention(proj, batch=batch, d_model=d)], odd_w_out[i].astype(BF16)
        x2 = _post_mixer(x2, ys, w_out, norm_ffn_g[layer], ffn_w_in[layer].astype(BF16),
                         ffn_w_out[layer].astype(BF16), final_g if layer == depth - 1 else None)
    return x2.reshape(batch, seq, d)
```

```python
import functools

import jax
import jax.numpy as jnp
from jax import lax
from jax.experimental import pallas as pl
from jax.experimental.pallas import tpu as pltpu

F32 = jnp.float32
BF16 = jnp.bfloat16

HEAD_DIM = 64
N_HEADS_A = 8
N_HEADS_B = 8
WIDTH_A = N_HEADS_A * HEAD_DIM
WIDTH_B = N_HEADS_B * HEAD_DIM
DILATIONS = (1, 4, 16)
SEG = 128
RET_HEADS = 4
RET_CHUNK = 128
RMS_EPS = 1e-6
GN_EPS = 1e-6
LANES = 128
NEG = -1e30
V7X_VMEM_LIMIT = 56 * 1024 * 1024

_NT = (((1,), (1,)), ((), ()))
_TN = (((0,), (0,)), ((), ()))


def _params(*sem):
    return pltpu.CompilerParams(dimension_semantics=sem, vmem_limit_bytes=V7X_VMEM_LIMIT)


def _resident(shape):
    nd = len(shape)
    return pl.BlockSpec(shape, lambda *_: (0,) * nd, pipeline_mode=pl.Buffered(1))


def _rms(x, g):
    ms = jnp.mean(x * x, axis=-1, keepdims=True)
    return x * lax.rsqrt(ms + RMS_EPS) * g


def _norm_proj_kernel(x_ref, g_ref, w_ref, *rest, tn, with_gate):
    if with_gate:
        wf_ref, o_ref, f_ref = rest
    else:
        (o_ref,) = rest
    h = _rms(x_ref[...], g_ref[...]).astype(BF16)
    for c in range(w_ref.shape[1] // tn):
        sl = slice(c * tn, (c + 1) * tn)
        o_ref[:, sl] = jnp.dot(h, w_ref[:, sl], preferred_element_type=F32).astype(o_ref.dtype)
    if with_gate:
        f_ref[...] = lax.dot_general(wf_ref[...], h, _NT, preferred_element_type=F32)


def _norm_proj(x2, g, w, wf_t=None, *, tm=512, tn=512):
    t, d = x2.shape
    n = w.shape[1]
    with_gate = wf_t is not None
    in_specs = [pl.BlockSpec((tm, d), lambda i: (i, 0)), _resident((1, d)), _resident((d, n))]
    out_specs = [pl.BlockSpec((tm, n), lambda i: (i, 0))]
    out_shape = [jax.ShapeDtypeStruct((t, n), BF16)]
    args = [x2, g.reshape(1, d), w]
    if with_gate:
        nh = wf_t.shape[0]
        in_specs.append(_resident((nh, d)))
        out_specs.append(pl.BlockSpec((nh, tm), lambda i: (0, i)))
        out_shape.append(jax.ShapeDtypeStruct((nh, t), F32))
        args.append(wf_t)
    res = pl.pallas_call(
        functools.partial(_norm_proj_kernel, tn=tn, with_gate=with_gate),
        grid=(t // tm,), in_specs=in_specs, out_specs=out_specs, out_shape=out_shape,
        compiler_params=_params("parallel"), name="norm_proj_gate" if with_gate else "norm_proj",
    )(*args)
    return res if with_gate else res[0]


def _gate_cumsum_kernel(f_ref, b_ref, c_ref):
    nh, s = f_ref.shape
    row = lax.broadcasted_iota(jnp.int32, (LANES, LANES), 0)
    col = lax.broadcasted_iota(jnp.int32, (LANES, LANES), 1)
    upper = (row <= col).astype(BF16)
    carry = jnp.zeros((nh, 1), F32)
    for n in range(s // LANES):
        sl = slice(n * LANES, (n + 1) * LANES)
        z = f_ref[:, sl] + b_ref[...]
        x = jnp.minimum(z, 0.0) - jnp.log1p(jnp.exp(-jnp.abs(z)))
        hi = x.astype(BF16)
        r1 = x - hi.astype(F32)
        mid = r1.astype(BF16)
        lo = (r1 - mid.astype(F32)).astype(BF16)
        c = (jnp.dot(hi, upper, preferred_element_type=F32)
             + jnp.dot(mid, upper, preferred_element_type=F32)
             + jnp.dot(lo, upper, preferred_element_type=F32)) + carry
        c_ref[:, sl] = c
        carry = c[:, LANES - 1:LANES]


def _gate_cumsum(f_t, b_f, *, batch):
    nh, t = f_t.shape
    s = t // batch
    return pl.pallas_call(
        _gate_cumsum_kernel, grid=(batch,),
        in_specs=[pl.BlockSpec((nh, s), lambda b: (0, b)), _resident((nh, 1))],
        out_specs=pl.BlockSpec((nh, s), lambda b: (0, b)),
        out_shape=jax.ShapeDtypeStruct((nh, t), F32),
        compiler_params=_params("parallel"), name="gate_cumsum",
    )(f_t, b_f.reshape(nh, 1).astype(F32))


def _dilated_kernel(slopes_ref, q_ref, k_ref, v_ref, o_ref,
                    qf, kf, vf, qs, ks, vs, ob, lb, bias):
    pair = pl.program_id(1)
    s = q_ref.shape[0]
    nblk = s // SEG
    lane_head = lax.broadcasted_iota(jnp.int32, (SEG, LANES), 1) // HEAD_DIM
    qi = lax.broadcasted_iota(jnp.int32, (SEG, 2 * SEG), 0)
    ki = lax.broadcasted_iota(jnp.int32, (SEG, 2 * SEG), 1)
    steps = qi + SEG - ki
    band = (steps >= 0) & (steps <= SEG)
    stepsf = steps.astype(F32)

    qf[...] = q_ref[...].astype(F32)
    kf[...] = k_ref[...].astype(F32)
    vf[...] = v_ref[...].astype(F32)
    ks[0:SEG, :] = jnp.zeros((SEG, LANES), BF16)
    vs[0:SEG, :] = jnp.zeros((SEG, LANES), BF16)

    for g, d in enumerate(DILATIONS):
        per_stream = nblk // d

        def stage(t, _, d=d, per_stream=per_stream):
            r = t // per_stream
            n = t - r * per_stream
            src = n * (SEG * d) + r
            dst = pl.multiple_of(t * SEG, SEG)
            rows = pl.ds(src, SEG, stride=d) if d > 1 else pl.ds(pl.multiple_of(src, SEG), SEG)
            qs[pl.ds(dst, SEG), :] = qf[rows, :].astype(BF16)
            ks[pl.ds(dst + SEG, SEG), :] = kf[rows, :].astype(BF16)
            vs[pl.ds(dst + SEG, SEG), :] = vf[rows, :].astype(BF16)
            return 0

        lax.fori_loop(0, nblk, stage, 0)

        for hh in range(2):
            slope = slopes_ref[2 * pair + hh] * float(d)
            b = jnp.where(band, -slope * stepsf, NEG)
            bias[2 * hh] = b
            bias[2 * hh + 1] = jnp.where(ki >= SEG, b, NEG)

        def tile(t, _, d=d, per_stream=per_stream, g=g):
            r = t // per_stream
            n = t - r * per_stream
            first = jnp.where(n == 0, 1, 0)
            base = pl.multiple_of(t * SEG, SEG)
            qt = qs[pl.ds(base, SEG), :]
            kk = ks[pl.ds(base, 2 * SEG), :]
            vv = vs[pl.ds(base, 2 * SEG), :]
            outs, lses = [], []
            for hh in range(2):
                qm = jnp.where(lane_head == hh, qt, jnp.zeros_like(qt))
                sc = lax.dot_general(qm, kk, _NT, preferred_element_type=F32)
                sc = sc + bias[2 * hh + first]
                m = jnp.max(sc, axis=-1, keepdims=True)
                p = jnp.exp(sc - m)
                den = jnp.sum(p, axis=-1, keepdims=True)
                acc = jnp.dot(p.astype(BF16), vv, preferred_element_type=F32)
                outs.append(acc / den)
                lses.append(m + jnp.log(den))
            o_t = jnp.where(lane_head == 0, outs[0], outs[1])
            l_t = jnp.where(lane_head == 0, lses[0], lses[1])
            dst = n * (SEG * d) + r
            rows = pl.ds(dst, SEG, stride=d) if d > 1 else pl.ds(pl.multiple_of(dst, SEG), SEG)
            ob[g, rows, :] = o_t
            lb[g, rows, :] = l_t
            return 0

        lax.fori_loop(0, nblk, tile, 0, unroll=4)

    def mix(n, _):
        rows = pl.ds(pl.multiple_of(n * SEG, SEG), SEG)
        l0, l1, l2 = lb[0, rows, :], lb[1, rows, :], lb[2, rows, :]
        m = jnp.maximum(jnp.maximum(l0, l1), l2)
        w0, w1, w2 = jnp.exp(l0 - m), jnp.exp(l1 - m), jnp.exp(l2 - m)
        num = w0 * ob[0, rows, :] + w1 * ob[1, rows, :] + w2 * ob[2, rows, :]
        o_ref[rows, :] = (num / (w0 + w1 + w2)).astype(o_ref.dtype)
        return 0

    lax.fori_loop(0, nblk, mix, 0)


def _dilated(qkv, slopes, *, batch):
    t, _ = qkv.shape
    s = t // batch
    assert s % (SEG * max(DILATIONS)) == 0
    pairs = WIDTH_A // LANES
    col = lambda off: pl.BlockSpec((s, LANES), lambda b, p: (b, off + p))
    return pl.pallas_call(
        _dilated_kernel, grid=(batch, pairs),
        in_specs=[pl.BlockSpec(memory_space=pltpu.SMEM), col(0), col(pairs), col(2 * pairs)],
        out_specs=pl.BlockSpec((s, LANES), lambda b, p: (b, p)),
        out_shape=jax.ShapeDtypeStruct((t, WIDTH_A), BF16),
        scratch_shapes=[pltpu.VMEM((s, LANES), F32)] * 3
        + [pltpu.VMEM((s, LANES), BF16), pltpu.VMEM((s + SEG, LANES), BF16), pltpu.VMEM((s + SEG, LANES), BF16)]
        + [pltpu.VMEM((len(DILATIONS), s, LANES), F32)] * 2
        + [pltpu.VMEM((4, SEG, 2 * SEG), F32)],
        compiler_params=_params("parallel", "parallel"), name="dilated",
    )(slopes, qkv, qkv, qkv)


C_TERMS = 3


def _bf16_terms(x):
    terms = []
    for _ in range(C_TERMS):
        t = x.astype(BF16).astype(F32)
        terms.append(t)
        x = x - t
    return terms


def _fox_kernel(c_ref, q_ref, k_ref, v_ref, o_ref, k2, vt, sa, sb, *, tq, tk):
    pair = pl.program_id(1)
    i = pl.program_id(2)
    s = k_ref.shape[0]
    lane = lax.broadcasted_iota(jnp.int32, (SEG, LANES), 1)

    @pl.when(i == 0)
    def _():
        sub = lax.broadcasted_iota(jnp.int32, (8, LANES), 0)

        def stage(n, _):
            rows = pl.ds(pl.multiple_of(n * SEG, SEG), SEG)
            kb = k_ref[rows, :]
            vt[:, rows] = v_ref[rows, :].astype(F32).T.astype(BF16)
            cb = c_ref[:, rows]
            for hh in range(2):
                ch = jnp.sum(jnp.where(sub == 2 * pair + hh, cb, 0.0), axis=0, keepdims=True)
                terms = _bf16_terms(ch)
                x8 = jnp.zeros((8, LANES), F32)
                for e, t in enumerate(terms):
                    x8 = jnp.where(sub == e, -t, x8)
                spare = HEAD_DIM * (1 - hh)
                pads = [spare, LANES - spare - 8]
                r = jnp.concatenate([jnp.zeros((pads[0], LANES), F32)] * (pads[0] > 0) + [x8]
                                    + [jnp.zeros((pads[1], LANES), F32)] * (pads[1] > 0), axis=0)
                k2[hh, rows, :] = jnp.where(lane // HEAD_DIM == hh, kb, r.T.astype(BF16))
            return 0

        lax.fori_loop(0, s // SEG, stage, 0)

    q = q_ref[...]
    qlane = lax.broadcasted_iota(jnp.int32, (tq, LANES), 1)
    qp = []
    for hh in range(2):
        spare = HEAD_DIM * (1 - hh)
        ones = ((qlane >= spare) & (qlane < spare + C_TERMS)).astype(BF16)
        qp.append(jnp.where(qlane // HEAD_DIM == hh, q, ones))

    nfull = (i * tq) // tk
    krow = lax.broadcasted_iota(jnp.int32, (tk, tq), 0) + (nfull * tk - i * tq)
    visible = krow <= lax.broadcasted_iota(jnp.int32, (tk, tq), 1)

    def scores(j, buf):
        cols = pl.ds(pl.multiple_of(j * tk, tk), tk)
        for hh in range(2):
            buf[hh] = lax.dot_general(k2[hh, cols, :], qp[hh], _NT, preferred_element_type=F32)

    def absorb(j, buf, stats, masked):
        cols = pl.ds(pl.multiple_of(j * tk, tk), tk)
        out = []
        for hh in range(2):
            m, l, acc = stats[hh]
            st = jnp.where(visible, buf[hh], NEG) if masked else buf[hh]
            m_new = jnp.maximum(m, jnp.max(st, axis=0, keepdims=True))
            alpha = jnp.exp(m - m_new)
            p = jnp.exp(st - m_new)
            l = alpha * l + jnp.sum(p, axis=0, keepdims=True)
            vth = vt[hh * HEAD_DIM:(hh + 1) * HEAD_DIM, cols]
            acc = alpha * acc + jnp.dot(vth, p.astype(BF16), preferred_element_type=F32)
            out.append((m_new, l, acc))
        return tuple(out)

    def pair_step(jj, stats):
        j = 2 * jj
        scores(j + 1, sb)
        stats = absorb(j, sa, stats, False)
        scores(j + 2, sa)
        return absorb(j + 1, sb, stats, False)

    def tail_even(stats):
        return absorb(nfull, sa, stats, True)

    def tail_odd(stats):
        scores(nfull, sb)
        stats = absorb(nfull - 1, sa, stats, False)
        return absorb(nfull, sb, stats, True)

    init = tuple((jnp.full((1, tq), NEG, F32), jnp.zeros((1, tq), F32), jnp.zeros((HEAD_DIM, tq), F32))
                 for _ in range(2))
    scores(0, sa)
    stats = lax.fori_loop(0, nfull // 2, pair_step, init)
    stats = lax.cond(nfull % 2 == 1, tail_odd, tail_even, stats)
    o_t = jnp.concatenate([acc / l for _, l, acc in stats], axis=0)
    o_ref[...] = o_t.T.astype(o_ref.dtype)


def _fox(qkv, c, *, batch, tq=256, tk=512):
    t, _ = qkv.shape
    s = t // batch
    nq = s // tq
    assert tk % tq == 0 and s % tk == 0
    pairs = WIDTH_B // LANES
    base = 3 * WIDTH_A // LANES
    kv = lambda off: pl.BlockSpec((s, LANES), lambda b, p, i: (b, base + off + p))
    return pl.pallas_call(
        functools.partial(_fox_kernel, tq=tq, tk=tk), grid=(batch, pairs, nq),
        in_specs=[pl.BlockSpec((N_HEADS_B, s), lambda b, p, i: (0, b)),
                  pl.BlockSpec((tq, LANES), lambda b, p, i: (b * nq + i, base + p)),
                  kv(pairs), kv(2 * pairs)],
        out_specs=pl.BlockSpec((tq, LANES), lambda b, p, i: (b * nq + i, p)),
        out_shape=jax.ShapeDtypeStruct((t, WIDTH_B), BF16),
        scratch_shapes=[pltpu.VMEM((2, s, LANES), BF16), pltpu.VMEM((LANES, s), BF16),
                        pltpu.VMEM((2, tk, tq), F32), pltpu.VMEM((2, tk, tq), F32)],
        compiler_params=_params("parallel", "parallel", "arbitrary"), name="fox",
    )(c, qkv, qkv, qkv)


def _retention_kernel(cdec_ref, q_ref, k_ref, v_ref, g_ref, din_ref, qdec_ref, kdec_ref, y_ref, state):
    head = pl.program_id(1)

    @pl.when(pl.program_id(2) == 0)
    def _():
        state[...] = jnp.zeros_like(state)

    chunk_dec = cdec_ref[head]
    c = RET_CHUNK
    for n in range(q_ref.shape[0] // c):
        rows = slice(n * c, (n + 1) * c)
        q = q_ref[rows, :]
        k = k_ref[rows, :]
        v = v_ref[rows, :]
        st = state[...]
        inner = lax.dot_general(q, k, _NT, preferred_element_type=F32) * din_ref[...]
        o = (jnp.dot(inner.astype(BF16), v, preferred_element_type=F32)
             + jnp.dot(q, st.astype(BF16), preferred_element_type=F32) * qdec_ref[...])
        kd = (k.astype(F32) * kdec_ref[...]).astype(BF16)
        state[...] = chunk_dec * st + lax.dot_general(kd, v, _TN, preferred_element_type=F32)
        mu = jnp.mean(o, axis=-1, keepdims=True)
        cen = o - mu
        var = jnp.mean(cen * cen, axis=-1, keepdims=True)
        gate = g_ref[rows, :].astype(F32)
        y_ref[rows, :] = (gate * jax.nn.sigmoid(gate) * (cen * lax.rsqrt(var + GN_EPS))).astype(y_ref.dtype)


def _retention(proj, *, batch, d_model, tb=1024):
    t, _ = proj.shape
    s = t // batch
    nb = s // tb
    dk = d_model // RET_HEADS
    dv = 2 * d_model // RET_HEADS
    c = RET_CHUNK
    log_g = jnp.log(1.0 - 2.0 ** (-5.0 - jnp.arange(RET_HEADS, dtype=F32)))
    idx = jnp.arange(c, dtype=F32)
    rel = idx[:, None] - idx[None, :]
    decay_in = jnp.where(rel >= 0, jnp.exp(log_g[:, None, None] * jnp.maximum(rel, 0.0)), 0.0)
    q_dec = jnp.exp(log_g[:, None] * (idx + 1.0))[:, :, None]
    k_dec = jnp.exp(log_g[:, None] * (c - 1.0 - idx))[:, :, None]
    chunk_dec = jnp.exp(log_g * c)
    per_head = lambda shape: pl.BlockSpec((None,) + shape, lambda b, h, i: (h, 0, 0))
    tok = lambda width, off: pl.BlockSpec((tb, width), lambda b, h, i: (b * nb + i, off + h))
    return pl.pallas_call(
        _retention_kernel, grid=(batch, RET_HEADS, nb),
        in_specs=[pl.BlockSpec(memory_space=pltpu.SMEM),
                  tok(dk, 0), tok(dk, d_model // dk), tok(dv, 2 * d_model // dv), tok(dv, 4 * d_model // dv),
                  per_head((c, c)), per_head((c, 1)), per_head((c, 1))],
        out_specs=tok(dv, 0),
        out_shape=jax.ShapeDtypeStruct((t, 2 * d_model), BF16),
        scratch_shapes=[pltpu.VMEM((dk, dv), F32)],
        compiler_params=_params("parallel", "parallel", "arbitrary"), name="retention",
    )(chunk_dec, proj, proj, proj, proj, decay_in, q_dec, k_dec)


def _post_mixer_kernel(*refs, n_in, th, final):
    x_ref = refs[0]
    ys = refs[1:1 + n_in]
    wo_ref, g_ref, w1_ref, w2_ref = refs[1 + n_in:5 + n_in]
    gf_ref = refs[5 + n_in] if final else None
    o_ref = refs[-1]
    y = jnp.concatenate([y_ref[...] for y_ref in ys], axis=1)
    x = x_ref[...] + jnp.dot(y, wo_ref[...], preferred_element_type=F32)
    h = _rms(x, g_ref[...]).astype(BF16)
    for c in range(w1_ref.shape[1] // th):
        sl = slice(c * th, (c + 1) * th)
        a = jnp.maximum(jnp.dot(h, w1_ref[:, sl], preferred_element_type=F32), 0.0)
        x = x + jnp.dot((a * a).astype(BF16), w2_ref[sl, :], preferred_element_type=F32)
    o_ref[...] = _rms(x, gf_ref[...]) if final else x


def _post_mixer(x2, ys, w_out, g, w1, w2, final_g=None, *, tm=512, th=512):
    t, d = x2.shape
    final = final_g is not None
    row = lambda width: pl.BlockSpec((tm, width), lambda i: (i, 0))
    in_specs = ([row(d)] + [row(y.shape[1]) for y in ys]
                + [_resident(w_out.shape), _resident((1, d)), _resident(w1.shape), _resident(w2.shape)])
    args = [x2, *ys, w_out, g.reshape(1, d), w1, w2]
    if final:
        in_specs.append(_resident((1, d)))
        args.append(final_g.reshape(1, d))
    return pl.pallas_call(
        functools.partial(_post_mixer_kernel, n_in=len(ys), th=th, final=final),
        grid=(t // tm,), in_specs=in_specs, out_specs=row(d),
        out_shape=jax.ShapeDtypeStruct((t, d), F32),
        compiler_params=_params("parallel"), name="post_mixer_final" if final else "post_mixer",
    )(*args)


def kernel(x, norm_mix_g, norm_ffn_g, even_w_in, even_b_f, even_w_out, odd_w_in, odd_w_out,
           ffn_w_in, ffn_w_out, final_g):
    batch, seq, d = x.shape
    depth = norm_mix_g.shape[0]
    x2 = x.reshape(batch * seq, d)
    slopes = 2.0 ** (-8.0 * (jnp.arange(N_HEADS_A, dtype=F32) + 1.0) / N_HEADS_A)
    n_qkv = 3 * WIDTH_A + 3 * WIDTH_B
    even_scale = jnp.ones((n_qkv,), F32).at[:WIDTH_A].set(HEAD_DIM ** -0.5)
    even_scale = even_scale.at[3 * WIDTH_A:3 * WIDTH_A + WIDTH_B].set(HEAD_DIM ** -0.5)
    odd_scale = jnp.ones((odd_w_in.shape[-1],), F32).at[d:2 * d].set((d // RET_HEADS) ** -0.5)
    for layer in range(depth):
        i = layer // 2
        if layer % 2 == 0:
            w = (even_w_in[i, :, :n_qkv] * even_scale).astype(BF16)
            wf_t = even_w_in[i, :, n_qkv:].T.astype(BF16)
            qkv, f_t = _norm_proj(x2, norm_mix_g[layer], w, wf_t)
            c = _gate_cumsum(f_t, even_b_f[i], batch=batch)
            oa = _dilated(qkv, slopes, batch=batch)
            ob = _fox(qkv, c, batch=batch)
            ys, w_out = [oa, ob], even_w_out[i].astype(BF16)
        else:
            w = (odd_w_in[i] * odd_scale).astype(BF16)
            proj = _norm_proj(x2, norm_mix_g[layer], w)
            ys, w_out = [_retention(proj, batch=batch, d_model=d)], odd_w_out[i].astype(BF16)
        x2 = _post_mixer(x2, ys, w_out, norm_ffn_g[layer], ffn_w_in[layer].astype(BF16),
                         ffn_w_out[layer].astype(BF16), final_g if layer == depth - 1 else None)
    return x2.reshape(batch, seq, d)
```

```python
import functools

import jax
import jax.numpy as jnp
from jax import lax
from jax.experimental import pallas as pl
from jax.experimental.pallas import tpu as pltpu

F32 = jnp.float32
BF16 = jnp.bfloat16

HEAD_DIM = 64
N_HEADS_A = 8
N_HEADS_B = 8
WIDTH_A = N_HEADS_A * HEAD_DIM
WIDTH_B = N_HEADS_B * HEAD_DIM
DILATIONS = (1, 4, 16)
SEG = 128
TILE_GROUP = 4
RET_HEADS = 4
RET_CHUNK = 128
RMS_EPS = 1e-6
GN_EPS = 1e-6
LANES = 128
NEG = -1e30
V7X_VMEM_LIMIT = 56 * 1024 * 1024

_NT = (((1,), (1,)), ((), ()))
_TN = (((0,), (0,)), ((), ()))


def _params(*sem):
    return pltpu.CompilerParams(dimension_semantics=sem, vmem_limit_bytes=V7X_VMEM_LIMIT)


def _resident(shape):
    nd = len(shape)
    return pl.BlockSpec(shape, lambda *_: (0,) * nd, pipeline_mode=pl.Buffered(1))


def _rms(x, g):
    ms = jnp.mean(x * x, axis=-1, keepdims=True)
    return x * lax.rsqrt(ms + RMS_EPS) * g


def _norm_proj_kernel(x_ref, g_ref, w_ref, *rest, tn, with_gate):
    if with_gate:
        wf_ref, o_ref, f_ref = rest
    else:
        (o_ref,) = rest
    h = _rms(x_ref[...], g_ref[...]).astype(BF16)
    for c in range(w_ref.shape[1] // tn):
        sl = slice(c * tn, (c + 1) * tn)
        o_ref[:, sl] = jnp.dot(h, w_ref[:, sl], preferred_element_type=F32).astype(o_ref.dtype)
    if with_gate:
        f_ref[...] = lax.dot_general(wf_ref[...], h, _NT, preferred_element_type=F32)


def _norm_proj(x2, g, w, wf_t=None, *, tm=512, tn=512):
    t, d = x2.shape
    n = w.shape[1]
    with_gate = wf_t is not None
    in_specs = [pl.BlockSpec((tm, d), lambda i: (i, 0)), _resident((1, d)), _resident((d, n))]
    out_specs = [pl.BlockSpec((tm, n), lambda i: (i, 0))]
    out_shape = [jax.ShapeDtypeStruct((t, n), BF16)]
    args = [x2, g.reshape(1, d), w]
    if with_gate:
        nh = wf_t.shape[0]
        in_specs.append(_resident((nh, d)))
        out_specs.append(pl.BlockSpec((nh, tm), lambda i: (0, i)))
        out_shape.append(jax.ShapeDtypeStruct((nh, t), F32))
        args.append(wf_t)
    res = pl.pallas_call(
        functools.partial(_norm_proj_kernel, tn=tn, with_gate=with_gate),
        grid=(t // tm,), in_specs=in_specs, out_specs=out_specs, out_shape=out_shape,
        compiler_params=_params("parallel"), name="norm_proj_gate" if with_gate else "norm_proj",
    )(*args)
    return res if with_gate else res[0]


def _gate_cumsum_kernel(f_ref, b_ref, c_ref):
    nh, s = f_ref.shape
    row = lax.broadcasted_iota(jnp.int32, (LANES, LANES), 0)
    col = lax.broadcasted_iota(jnp.int32, (LANES, LANES), 1)
    upper = (row <= col).astype(BF16)
    carry = jnp.zeros((nh, 1), F32)
    for n in range(s // LANES):
        sl = slice(n * LANES, (n + 1) * LANES)
        z = f_ref[:, sl] + b_ref[...]
        x = jnp.minimum(z, 0.0) - jnp.log1p(jnp.exp(-jnp.abs(z)))
        hi = x.astype(BF16)
        r1 = x - hi.astype(F32)
        mid = r1.astype(BF16)
        lo = (r1 - mid.astype(F32)).astype(BF16)
        c = (jnp.dot(hi, upper, preferred_element_type=F32)
             + jnp.dot(mid, upper, preferred_element_type=F32)
             + jnp.dot(lo, upper, preferred_element_type=F32)) + carry
        c_ref[:, sl] = c
        carry = c[:, LANES - 1:LANES]


def _gate_cumsum(f_t, b_f, *, batch):
    nh, t = f_t.shape
    s = t // batch
    return pl.pallas_call(
        _gate_cumsum_kernel, grid=(batch,),
        in_specs=[pl.BlockSpec((nh, s), lambda b: (0, b)), _resident((nh, 1))],
        out_specs=pl.BlockSpec((nh, s), lambda b: (0, b)),
        out_shape=jax.ShapeDtypeStruct((nh, t), F32),
        compiler_params=_params("parallel"), name="gate_cumsum",
    )(f_t, b_f.reshape(nh, 1).astype(F32))


def _dilated_kernel(slopes_ref, q_ref, k_ref, v_ref, o_ref,
                    qf, kf, vf, qs, ks, vts, ob, lb, bias, sa, sb):
    pair = pl.program_id(1)
    s = q_ref.shape[0]
    nblk = s // SEG
    lane_head = lax.broadcasted_iota(jnp.int32, (SEG, LANES), 1) // HEAD_DIM
    ki = lax.broadcasted_iota(jnp.int32, (2 * SEG, SEG), 0)
    qi = lax.broadcasted_iota(jnp.int32, (2 * SEG, SEG), 1)
    steps = qi + SEG - ki
    band = (steps >= 0) & (steps <= SEG)
    stepsf = steps.astype(F32)

    qf[...] = q_ref[...].astype(F32)
    kf[...] = k_ref[...].astype(F32)
    vf[...] = v_ref[...].astype(F32)
    ks[0:SEG, :] = jnp.zeros((SEG, LANES), BF16)
    vts[:, 0:SEG] = jnp.zeros((LANES, SEG), BF16)

    for g, d in enumerate(DILATIONS):
        per_stream = nblk // d

        def stage(t, _, d=d, per_stream=per_stream):
            r = t // per_stream
            n = t - r * per_stream
            src = n * (SEG * d) + r
            dst = pl.multiple_of(t * SEG, SEG)
            rows = pl.ds(src, SEG, stride=d) if d > 1 else pl.ds(pl.multiple_of(src, SEG), SEG)
            qs[pl.ds(dst, SEG), :] = qf[rows, :].astype(BF16)
            ks[pl.ds(dst + SEG, SEG), :] = kf[rows, :].astype(BF16)
            vts[:, pl.ds(dst + SEG, SEG)] = vf[rows, :].T.astype(BF16)
            return 0

        lax.fori_loop(0, nblk, stage, 0, unroll=8)

        for hh in range(2):
            slope = slopes_ref[2 * pair + hh] * float(d)
            b = jnp.where(band, -slope * stepsf, NEG)
            bias[2 * hh] = b
            bias[2 * hh + 1] = jnp.where(ki >= SEG, b, NEG)

        def scores(grp, buf):
            for a in range(TILE_GROUP):
                base = pl.multiple_of((grp * TILE_GROUP + a) * SEG, SEG)
                qt = qs[pl.ds(base, SEG), :]
                kk = ks[pl.ds(base, 2 * SEG), :]
                for hh in range(2):
                    qm = jnp.where(lane_head == hh, qt, jnp.zeros_like(qt))
                    buf[2 * a + hh] = lax.dot_general(kk, qm, _NT, preferred_element_type=F32)

        def absorb(grp, buf, d=d, per_stream=per_stream, g=g):
            for a in range(TILE_GROUP):
                t = grp * TILE_GROUP + a
                r = t // per_stream
                n = t - r * per_stream
                first = jnp.where(n == 0, 1, 0)
                base = pl.multiple_of(t * SEG, SEG)
                o_t, lses = [], []
                for hh in range(2):
                    st = buf[2 * a + hh] + bias[2 * hh + first]
                    m = jnp.max(st, axis=0, keepdims=True)
                    p = jnp.exp(st - m)
                    den = jnp.sum(p, axis=0, keepdims=True)
                    vth = vts[hh * HEAD_DIM:(hh + 1) * HEAD_DIM, pl.ds(base, 2 * SEG)]
                    acc = jnp.dot(vth, p.astype(BF16), preferred_element_type=F32)
                    o_t.append(acc / den)
                    lses.append(m + jnp.log(den))
                dst = n * (SEG * d) + r
                rows = pl.ds(dst, SEG, stride=d) if d > 1 else pl.ds(pl.multiple_of(dst, SEG), SEG)
                ob[g, rows, :] = jnp.concatenate(o_t, axis=0).T
                lb[g, rows, :] = jnp.concatenate(
                    [jnp.broadcast_to(l, (HEAD_DIM, SEG)) for l in lses], axis=0).T

        ngroups = nblk // TILE_GROUP

        def group_pair(u, _):
            scores(2 * u + 1, sb)
            absorb(2 * u, sa)
            scores(jnp.minimum(2 * u + 2, ngroups - 1), sa)
            absorb(2 * u + 1, sb)
            return 0

        scores(0, sa)
        lax.fori_loop(0, ngroups // 2, group_pair, 0)

    def mix(n, _):
        rows = pl.ds(pl.multiple_of(n * SEG, SEG), SEG)
        l0, l1, l2 = lb[0, rows, :], lb[1, rows, :], lb[2, rows, :]
        m = jnp.maximum(jnp.maximum(l0, l1), l2)
        w0, w1, w2 = jnp.exp(l0 - m), jnp.exp(l1 - m), jnp.exp(l2 - m)
        num = w0 * ob[0, rows, :] + w1 * ob[1, rows, :] + w2 * ob[2, rows, :]
        o_ref[rows, :] = (num / (w0 + w1 + w2)).astype(o_ref.dtype)
        return 0

    lax.fori_loop(0, nblk, mix, 0, unroll=2)


def _dilated(qkv, slopes, *, batch):
    t, _ = qkv.shape
    s = t // batch
    assert s % (SEG * max(DILATIONS)) == 0
    pairs = WIDTH_A // LANES
    col = lambda off: pl.BlockSpec((s, LANES), lambda b, p: (b, off + p))
    return pl.pallas_call(
        _dilated_kernel, grid=(batch, pairs),
        in_specs=[pl.BlockSpec(memory_space=pltpu.SMEM), col(0), col(pairs), col(2 * pairs)],
        out_specs=pl.BlockSpec((s, LANES), lambda b, p: (b, p)),
        out_shape=jax.ShapeDtypeStruct((t, WIDTH_A), BF16),
        scratch_shapes=[pltpu.VMEM((s, LANES), F32)] * 3
        + [pltpu.VMEM((s, LANES), BF16), pltpu.VMEM((s + SEG, LANES), BF16), pltpu.VMEM((LANES, s + SEG), BF16)]
        + [pltpu.VMEM((len(DILATIONS), s, LANES), F32)] * 2
        + [pltpu.VMEM((4, 2 * SEG, SEG), F32)]
        + [pltpu.VMEM((2 * TILE_GROUP, 2 * SEG, SEG), F32)] * 2,
        compiler_params=_params("parallel", "parallel"), name="dilated",
    )(slopes, qkv, qkv, qkv)


C_TERMS = 3


def _bf16_terms(x):
    terms = []
    for _ in range(C_TERMS):
        t = x.astype(BF16).astype(F32)
        terms.append(t)
        x = x - t
    return terms


def _fox_kernel(c_ref, q_ref, k_ref, v_ref, o_ref, k2, vt, sa, sb, *, tq, tk):
    pair = pl.program_id(1)
    i = pl.program_id(2)
    s = k_ref.shape[0]
    lane = lax.broadcasted_iota(jnp.int32, (SEG, LANES), 1)

    @pl.when(i == 0)
    def _():
        sub = lax.broadcasted_iota(jnp.int32, (8, LANES), 0)

        def stage(n, _):
            rows = pl.ds(pl.multiple_of(n * SEG, SEG), SEG)
            kb = k_ref[rows, :]
            vt[:, rows] = v_ref[rows, :].astype(F32).T.astype(BF16)
            cb = c_ref[:, rows]
            x8 = []
            for hh in range(2):
                ch = jnp.sum(jnp.where(sub == 2 * pair + hh, cb, 0.0), axis=0, keepdims=True)
                x = jnp.zeros((8, LANES), F32)
                for e, t in enumerate(_bf16_terms(ch)):
                    x = jnp.where(sub == e, -t, x)
                x8.append(x)
            gap = jnp.zeros((HEAD_DIM - 8, LANES), F32)
            rt = jnp.concatenate([x8[1], gap, x8[0], gap], axis=0).T.astype(BF16)
            for hh in range(2):
                k2[hh, rows, :] = jnp.where(lane // HEAD_DIM == hh, kb, rt)
            return 0

        lax.fori_loop(0, s // SEG, stage, 0, unroll=2)

    q = q_ref[...]
    qlane = lax.broadcasted_iota(jnp.int32, (tq, LANES), 1)
    qp = []
    for hh in range(2):
        spare = HEAD_DIM * (1 - hh)
        ones = ((qlane >= spare) & (qlane < spare + C_TERMS)).astype(BF16)
        qp.append(jnp.where(qlane // HEAD_DIM == hh, q, ones))

    nfull = (i * tq) // tk
    krow = lax.broadcasted_iota(jnp.int32, (tk, tq), 0) + (nfull * tk - i * tq)
    visible = krow <= lax.broadcasted_iota(jnp.int32, (tk, tq), 1)

    def scores(j, buf):
        cols = pl.ds(pl.multiple_of(j * tk, tk), tk)
        for hh in range(2):
            buf[hh] = lax.dot_general(k2[hh, cols, :], qp[hh], _NT, preferred_element_type=F32)

    def absorb(j, buf, stats, masked):
        cols = pl.ds(pl.multiple_of(j * tk, tk), tk)
        out = []
        for hh in range(2):
            m, l, acc = stats[hh]
            st = jnp.where(visible, buf[hh], NEG) if masked else buf[hh]
            m_new = jnp.maximum(m, jnp.max(st, axis=0, keepdims=True))
            alpha = jnp.exp(m - m_new)
            p = jnp.exp(st - m_new)
            l = alpha * l + jnp.sum(p, axis=0, keepdims=True)
            vth = vt[hh * HEAD_DIM:(hh + 1) * HEAD_DIM, cols]
            acc = alpha * acc + jnp.dot(vth, p.astype(BF16), preferred_element_type=F32)
            out.append((m_new, l, acc))
        return tuple(out)

    def pair_step(t, stats):
        j = 2 * t
        scores(j + 1, sa)
        stats = absorb(j, sb, stats, False)
        scores(j + 2, sb)
        return absorb(j + 1, sa, stats, False)

    init = tuple((jnp.full((1, tq), NEG, F32), jnp.zeros((1, tq), F32), jnp.zeros((HEAD_DIM, tq), F32))
                 for _ in range(2))
    scores(nfull, sa)
    scores(0, sb)
    stats = absorb(nfull, sa, init, True)
    stats = lax.fori_loop(0, nfull // 2, pair_step, stats)
    stats = lax.cond(nfull % 2 == 1, lambda st: absorb(nfull - 1, sb, st, False), lambda st: st, stats)
    o_t = jnp.concatenate([acc / l for _, l, acc in stats], axis=0)
    o_ref[...] = o_t.T.astype(o_ref.dtype)


def _fox(qkv, c, *, batch, tq=512, tk=512):
    t, _ = qkv.shape
    s = t // batch
    nq = s // tq
    assert tk % tq == 0 and s % tk == 0
    pairs = WIDTH_B // LANES
    base = 3 * WIDTH_A // LANES
    kv = lambda off: pl.BlockSpec((s, LANES), lambda b, p, i: (b, base + off + p))
    return pl.pallas_call(
        functools.partial(_fox_kernel, tq=tq, tk=tk), grid=(batch, pairs, nq),
        in_specs=[pl.BlockSpec((N_HEADS_B, s), lambda b, p, i: (0, b)),
                  pl.BlockSpec((tq, LANES), lambda b, p, i: (b * nq + i, base + p)),
                  kv(pairs), kv(2 * pairs)],
        out_specs=pl.BlockSpec((tq, LANES), lambda b, p, i: (b * nq + i, p)),
        out_shape=jax.ShapeDtypeStruct((t, WIDTH_B), BF16),
        scratch_shapes=[pltpu.VMEM((2, s, LANES), BF16), pltpu.VMEM((LANES, s), BF16),
                        pltpu.VMEM((2, tk, tq), F32), pltpu.VMEM((2, tk, tq), F32)],
        compiler_params=_params("parallel", "parallel", "arbitrary"), name="fox",
    )(c, qkv, qkv, qkv)


def _retention_kernel(cdec_ref, q_ref, k_ref, v_ref, g_ref, din_ref, qdec_ref, kdec_ref, y_ref, state):
    head = pl.program_id(1)

    @pl.when(pl.program_id(2) == 0)
    def _():
        state[...] = jnp.zeros_like(state)

    chunk_dec = cdec_ref[head]
    c = RET_CHUNK
    for n in range(q_ref.shape[0] // c):
        rows = slice(n * c, (n + 1) * c)
        q = q_ref[rows, :]
        k = k_ref[rows, :]
        v = v_ref[rows, :]
        st = state[...]
        inner = lax.dot_general(q, k, _NT, preferred_element_type=F32) * din_ref[...]
        o = (jnp.dot(inner.astype(BF16), v, preferred_element_type=F32)
             + jnp.dot(q, st.astype(BF16), preferred_element_type=F32) * qdec_ref[...])
        kd = (k.astype(F32) * kdec_ref[...]).astype(BF16)
        state[...] = chunk_dec * st + lax.dot_general(kd, v, _TN, preferred_element_type=F32)
        mu = jnp.mean(o, axis=-1, keepdims=True)
        cen = o - mu
        var = jnp.mean(cen * cen, axis=-1, keepdims=True)
        gate = g_ref[rows, :].astype(F32)
        y_ref[rows, :] = (gate * jax.nn.sigmoid(gate) * (cen * lax.rsqrt(var + GN_EPS))).astype(y_ref.dtype)


def _retention(proj, *, batch, d_model, tb=1024):
    t, _ = proj.shape
    s = t // batch
    nb = s // tb
    dk = d_model // RET_HEADS
    dv = 2 * d_model // RET_HEADS
    c = RET_CHUNK
    log_g = jnp.log(1.0 - 2.0 ** (-5.0 - jnp.arange(RET_HEADS, dtype=F32)))
    idx = jnp.arange(c, dtype=F32)
    rel = idx[:, None] - idx[None, :]
    decay_in = jnp.where(rel >= 0, jnp.exp(log_g[:, None, None] * jnp.maximum(rel, 0.0)), 0.0)
    q_dec = jnp.exp(log_g[:, None] * (idx + 1.0))[:, :, None]
    k_dec = jnp.exp(log_g[:, None] * (c - 1.0 - idx))[:, :, None]
    chunk_dec = jnp.exp(log_g * c)
    per_head = lambda shape: pl.BlockSpec((None,) + shape, lambda b, h, i: (h, 0, 0))
    tok = lambda width, off: pl.BlockSpec((tb, width), lambda b, h, i: (b * nb + i, off + h))
    return pl.pallas_call(
        _retention_kernel, grid=(batch, RET_HEADS, nb),
        in_specs=[pl.BlockSpec(memory_space=pltpu.SMEM),
                  tok(dk, 0), tok(dk, d_model // dk), tok(dv, 2 * d_model // dv), tok(dv, 4 * d_model // dv),
                  per_head((c, c)), per_head((c, 1)), per_head((c, 1))],
        out_specs=tok(dv, 0),
        out_shape=jax.ShapeDtypeStruct((t, 2 * d_model), BF16),
        scratch_shapes=[pltpu.VMEM((dk, dv), F32)],
        compiler_params=_params("parallel", "parallel", "arbitrary"), name="retention",
    )(chunk_dec, proj, proj, proj, proj, decay_in, q_dec, k_dec)


def _post_mixer_kernel(*refs, n_in, th, final):
    x_ref = refs[0]
    ys = refs[1:1 + n_in]
    wo_ref, g_ref, w1_ref, w2_ref = refs[1 + n_in:5 + n_in]
    gf_ref = refs[5 + n_in] if final else None
    o_ref = refs[-1]
    y = jnp.concatenate([y_ref[...] for y_ref in ys], axis=1)
    x = x_ref[...] + jnp.dot(y, wo_ref[...], preferred_element_type=F32)
    h = _rms(x, g_ref[...]).astype(BF16)
    for c in range(w1_ref.shape[1] // th):
        sl = slice(c * th, (c + 1) * th)
        a = jnp.maximum(jnp.dot(h, w1_ref[:, sl], preferred_element_type=F32), 0.0)
        x = x + jnp.dot((a * a).astype(BF16), w2_ref[sl, :], preferred_element_type=F32)
    o_ref[...] = _rms(x, gf_ref[...]) if final else x


def _post_mixer(x2, ys, w_out, g, w1, w2, final_g=None, *, tm=512, th=512):
    t, d = x2.shape
    final = final_g is not None
    row = lambda width: pl.BlockSpec((tm, width), lambda i: (i, 0))
    in_specs = ([row(d)] + [row(y.shape[1]) for y in ys]
                + [_resident(w_out.shape), _resident((1, d)), _resident(w1.shape), _resident(w2.shape)])
    args = [x2, *ys, w_out, g.reshape(1, d), w1, w2]
    if final:
        in_specs.append(_resident((1, d)))
        args.append(final_g.reshape(1, d))
    return pl.pallas_call(
        functools.partial(_post_mixer_kernel, n_in=len(ys), th=th, final=final),
        grid=(t // tm,), in_specs=in_specs, out_specs=row(d),
        out_shape=jax.ShapeDtypeStruct((t, d), F32),
        compiler_params=_params("parallel"), name="post_mixer_final" if final else "post_mixer",
    )(*args)


def kernel(x, norm_mix_g, norm_ffn_g, even_w_in, even_b_f, even_w_out, odd_w_in, odd_w_out,
           ffn_w_in, ffn_w_out, final_g):
    batch, seq, d = x.shape
    depth = norm_mix_g.shape[0]
    x2 = x.reshape(batch * seq, d)
    slopes = 2.0 ** (-8.0 * (jnp.arange(N_HEADS_A, dtype=F32) + 1.0) / N_HEADS_A)
    n_qkv = 3 * WIDTH_A + 3 * WIDTH_B
    even_scale = jnp.ones((n_qkv,), F32).at[:WIDTH_A].set(HEAD_DIM ** -0.5)
    even_scale = even_scale.at[3 * WIDTH_A:3 * WIDTH_A + WIDTH_B].set(HEAD_DIM ** -0.5)
    odd_scale = jnp.ones((odd_w_in.shape[-1],), F32).at[d:2 * d].set((d // RET_HEADS) ** -0.5)
    for layer in range(depth):
        i = layer // 2
        if layer % 2 == 0:
            w = (even_w_in[i, :, :n_qkv] * even_scale).astype(BF16)
            wf_t = even_w_in[i, :, n_qkv:].T.astype(BF16)
            qkv, f_t = _norm_proj(x2, norm_mix_g[layer], w, wf_t)
            c = _gate_cumsum(f_t, even_b_f[i], batch=batch)
            oa = _dilated(qkv, slopes, batch=batch)
            ob = _fox(qkv, c, batch=batch)
            ys, w_out = [oa, ob], even_w_out[i].astype(BF16)
        else:
            w = (odd_w_in[i] * odd_scale).astype(BF16)
            proj = _norm_proj(x2, norm_mix_g[layer], w)
            ys, w_out = [_retention(proj, batch=batch, d_model=d)], odd_w_out[i].astype(BF16)
        x2 = _post_mixer(x2, ys, w_out, norm_ffn_g[layer], ffn_w_in[layer].astype(BF16),
                         ffn_w_out[layer].astype(BF16), final_g if layer == depth - 1 else None)
    return x2.reshape(batch, seq, d)
```

```python
import functools

import jax
import jax.numpy as jnp
from jax import lax
from jax.experimental import pallas as pl
from jax.experimental.pallas import tpu as pltpu

F32 = jnp.float32
BF16 = jnp.bfloat16

HEAD_DIM = 64
N_HEADS_A = 8
N_HEADS_B = 8
WIDTH_A = N_HEADS_A * HEAD_DIM
WIDTH_B = N_HEADS_B * HEAD_DIM
DILATIONS = (1, 4, 16)
SEG = 128
TILE_GROUP = 4
RET_HEADS = 4
RET_CHUNK = 128
RMS_EPS = 1e-6
GN_EPS = 1e-6
LANES = 128
NEG = -1e30
LOG2E = 1.4426950408889634
V7X_VMEM_LIMIT = 56 * 1024 * 1024

_NT = (((1,), (1,)), ((), ()))
_TN = (((0,), (0,)), ((), ()))


def _params(*sem):
    return pltpu.CompilerParams(dimension_semantics=sem, vmem_limit_bytes=V7X_VMEM_LIMIT)


def _resident(shape):
    nd = len(shape)
    return pl.BlockSpec(shape, lambda *_: (0,) * nd, pipeline_mode=pl.Buffered(1))


def _rms(x, g):
    ms = jnp.mean(x * x, axis=-1, keepdims=True)
    return x * lax.rsqrt(ms + RMS_EPS) * g


def _norm_proj_kernel(x_ref, g_ref, w_ref, *rest, tn, with_gate):
    if with_gate:
        wf_ref, o_ref, f_ref = rest
    else:
        (o_ref,) = rest
    h = _rms(x_ref[...], g_ref[...]).astype(BF16)
    for c in range(w_ref.shape[1] // tn):
        sl = slice(c * tn, (c + 1) * tn)
        o_ref[:, sl] = jnp.dot(h, w_ref[:, sl], preferred_element_type=F32).astype(o_ref.dtype)
    if with_gate:
        f_ref[...] = lax.dot_general(wf_ref[...], h, _NT, preferred_element_type=F32)


def _norm_proj(x2, g, w, wf_t=None, *, tm=512, tn=512):
    t, d = x2.shape
    n = w.shape[1]
    with_gate = wf_t is not None
    in_specs = [pl.BlockSpec((tm, d), lambda i: (i, 0)), _resident((1, d)), _resident((d, n))]
    out_specs = [pl.BlockSpec((tm, n), lambda i: (i, 0))]
    out_shape = [jax.ShapeDtypeStruct((t, n), BF16)]
    args = [x2, g.reshape(1, d), w]
    if with_gate:
        nh = wf_t.shape[0]
        in_specs.append(_resident((nh, d)))
        out_specs.append(pl.BlockSpec((nh, tm), lambda i: (0, i)))
        out_shape.append(jax.ShapeDtypeStruct((nh, t), F32))
        args.append(wf_t)
    res = pl.pallas_call(
        functools.partial(_norm_proj_kernel, tn=tn, with_gate=with_gate),
        grid=(t // tm,), in_specs=in_specs, out_specs=out_specs, out_shape=out_shape,
        compiler_params=_params("parallel"), name="norm_proj_gate" if with_gate else "norm_proj",
    )(*args)
    return res if with_gate else res[0]


def _gate_cumsum_kernel(f_ref, b_ref, c_ref):
    nh, s = f_ref.shape
    row = lax.broadcasted_iota(jnp.int32, (LANES, LANES), 0)
    col = lax.broadcasted_iota(jnp.int32, (LANES, LANES), 1)
    upper = (row <= col).astype(BF16)
    carry = jnp.zeros((nh, 1), F32)
    for n in range(s // LANES):
        sl = slice(n * LANES, (n + 1) * LANES)
        z = f_ref[:, sl] + b_ref[...]
        x = jnp.minimum(z, 0.0) - jnp.log1p(jnp.exp(-jnp.abs(z)))
        hi = x.astype(BF16)
        r1 = x - hi.astype(F32)
        mid = r1.astype(BF16)
        lo = (r1 - mid.astype(F32)).astype(BF16)
        c = (jnp.dot(hi, upper, preferred_element_type=F32)
             + jnp.dot(mid, upper, preferred_element_type=F32)
             + jnp.dot(lo, upper, preferred_element_type=F32)) + carry
        c_ref[:, sl] = c
        carry = c[:, LANES - 1:LANES]


def _gate_cumsum(f_t, b_f, *, batch):
    nh, t = f_t.shape
    s = t // batch
    return pl.pallas_call(
        _gate_cumsum_kernel, grid=(batch,),
        in_specs=[pl.BlockSpec((nh, s), lambda b: (0, b)), _resident((nh, 1))],
        out_specs=pl.BlockSpec((nh, s), lambda b: (0, b)),
        out_shape=jax.ShapeDtypeStruct((nh, t), F32),
        compiler_params=_params("parallel"), name="gate_cumsum",
    )(f_t, b_f.reshape(nh, 1).astype(F32))


def _dilated_kernel(slopes_ref, q_ref, k_ref, v_ref, o_ref,
                    qf, kf, vf, qg, kg, vg, qs, ks, vts, ob, lb, bias, sa, sb):
    pair = pl.program_id(1)
    s = q_ref.shape[0]
    nblk = s // SEG
    lane_head = lax.broadcasted_iota(jnp.int32, (SEG, LANES), 1) // HEAD_DIM
    ki = lax.broadcasted_iota(jnp.int32, (2 * SEG, SEG), 0)
    qi = lax.broadcasted_iota(jnp.int32, (2 * SEG, SEG), 1)
    steps = qi + SEG - ki
    band = (steps >= 0) & (steps <= SEG)
    stepsf = steps.astype(F32)

    qf[...] = q_ref[...].astype(F32)
    kf[...] = k_ref[...].astype(F32)
    vf[...] = v_ref[...].astype(F32)
    ks[0:SEG, :] = jnp.zeros((SEG, LANES), BF16)
    vts[:, 0:SEG] = jnp.zeros((LANES, SEG), BF16)

    f32_src = (qf, kf, vf)
    d_src = 1
    for g, d in enumerate(DILATIONS):
        per_stream = nblk // d
        ratio = d // d_src
        keep = g + 1 < len(DILATIONS) and d > 1
        f32_dst = (qg, kg, vg) if keep else None

        def stage(t, _, d=d, per_stream=per_stream, ratio=ratio, d_src=d_src, f32_src=f32_src, f32_dst=f32_dst):
            r = t // per_stream
            n = t - r * per_stream
            src = (r % d_src) * (s // d_src) + r // d_src + n * (SEG * ratio)
            dst = pl.multiple_of(t * SEG, SEG)
            rows = pl.ds(src, SEG, stride=ratio) if ratio > 1 else pl.ds(pl.multiple_of(src, SEG), SEG)
            q, k, v = (ref[rows, :] for ref in f32_src)
            if f32_dst is not None:
                for ref, val in zip(f32_dst, (q, k, v)):
                    ref[pl.ds(dst, SEG), :] = val
            qs[pl.ds(dst, SEG), :] = q.astype(BF16)
            ks[pl.ds(dst + SEG, SEG), :] = k.astype(BF16)
            vts[:, pl.ds(dst + SEG, SEG)] = v.T.astype(BF16)
            return 0

        lax.fori_loop(0, nblk, stage, 0, unroll=8)
        if keep:
            f32_src, d_src = f32_dst, d

        for hh in range(2):
            slope = slopes_ref[2 * pair + hh] * (d * LOG2E)
            b = jnp.where(band, -slope * stepsf, NEG)
            bias[2 * hh] = b
            bias[2 * hh + 1] = jnp.where(ki >= SEG, b, NEG)

        def scores(grp, buf):
            for a in range(TILE_GROUP):
                base = pl.multiple_of((grp * TILE_GROUP + a) * SEG, SEG)
                qt = qs[pl.ds(base, SEG), :]
                kk = ks[pl.ds(base, 2 * SEG), :]
                for hh in range(2):
                    qm = jnp.where(lane_head == hh, qt, jnp.zeros_like(qt))
                    buf[2 * a + hh] = lax.dot_general(kk, qm, _NT, preferred_element_type=F32)

        def absorb(grp, buf, d=d, per_stream=per_stream, g=g):
            for a in range(TILE_GROUP):
                t = grp * TILE_GROUP + a
                r = t // per_stream
                n = t - r * per_stream
                first = jnp.where(n == 0, 1, 0)
                base = pl.multiple_of(t * SEG, SEG)
                o_t, lses = [], []
                for hh in range(2):
                    st = buf[2 * a + hh] + bias[2 * hh + first]
                    m = jnp.max(st, axis=0, keepdims=True)
                    p = jnp.exp2(st - m)
                    den = jnp.sum(p, axis=0, keepdims=True)
                    vth = vts[hh * HEAD_DIM:(hh + 1) * HEAD_DIM, pl.ds(base, 2 * SEG)]
                    acc = jnp.dot(vth, p.astype(BF16), preferred_element_type=F32)
                    o_t.append(acc / den)
                    lses.append(m + jnp.log2(den))
                dst = n * (SEG * d) + r
                rows = pl.ds(dst, SEG, stride=d) if d > 1 else pl.ds(pl.multiple_of(dst, SEG), SEG)
                ob[g, rows, :] = jnp.concatenate(o_t, axis=0).T
                lb[g, rows, :] = jnp.concatenate(
                    [jnp.broadcast_to(l, (HEAD_DIM, SEG)) for l in lses], axis=0).T

        ngroups = nblk // TILE_GROUP

        def group_pair(u, _):
            scores(2 * u + 1, sb)
            absorb(2 * u, sa)
            scores(jnp.minimum(2 * u + 2, ngroups - 1), sa)
            absorb(2 * u + 1, sb)
            return 0

        scores(0, sa)
        lax.fori_loop(0, ngroups // 2, group_pair, 0)

    def mix(n, _):
        rows = pl.ds(pl.multiple_of(n * SEG, SEG), SEG)
        l0, l1, l2 = lb[0, rows, :], lb[1, rows, :], lb[2, rows, :]
        m = jnp.maximum(jnp.maximum(l0, l1), l2)
        w0, w1, w2 = jnp.exp2(l0 - m), jnp.exp2(l1 - m), jnp.exp2(l2 - m)
        num = w0 * ob[0, rows, :] + w1 * ob[1, rows, :] + w2 * ob[2, rows, :]
        o_ref[rows, :] = (num / (w0 + w1 + w2)).astype(o_ref.dtype)
        return 0

    lax.fori_loop(0, nblk, mix, 0, unroll=2)


def _dilated(qkv, slopes, *, batch):
    t, _ = qkv.shape
    s = t // batch
    assert s % (SEG * max(DILATIONS)) == 0
    pairs = WIDTH_A // LANES
    col = lambda off: pl.BlockSpec((s, LANES), lambda b, p: (b, off + p))
    return pl.pallas_call(
        _dilated_kernel, grid=(batch, pairs),
        in_specs=[pl.BlockSpec(memory_space=pltpu.SMEM), col(0), col(pairs), col(2 * pairs)],
        out_specs=pl.BlockSpec((s, LANES), lambda b, p: (b, p)),
        out_shape=jax.ShapeDtypeStruct((t, WIDTH_A), BF16),
        scratch_shapes=[pltpu.VMEM((s, LANES), F32)] * 6
        + [pltpu.VMEM((s, LANES), BF16), pltpu.VMEM((s + SEG, LANES), BF16), pltpu.VMEM((LANES, s + SEG), BF16)]
        + [pltpu.VMEM((len(DILATIONS), s, LANES), F32)] * 2
        + [pltpu.VMEM((4, 2 * SEG, SEG), F32)]
        + [pltpu.VMEM((2 * TILE_GROUP, 2 * SEG, SEG), F32)] * 2,
        compiler_params=_params("parallel", "parallel"), name="dilated",
    )(slopes, qkv, qkv, qkv)


C_TERMS = 3


def _bf16_terms(x):
    terms = []
    for _ in range(C_TERMS):
        t = x.astype(BF16).astype(F32)
        terms.append(t)
        x = x - t
    return terms


def _fox_kernel(c_ref, q_ref, k_ref, v_ref, o_ref, k2, vt, sa, sb, *, tq, tk):
    pair = pl.program_id(1)
    i = pl.program_id(2)
    s = k_ref.shape[0]
    lane = lax.broadcasted_iota(jnp.int32, (SEG, LANES), 1)

    @pl.when(i == 0)
    def _():
        sub = lax.broadcasted_iota(jnp.int32, (8, LANES), 0)

        def stage(n, _):
            rows = pl.ds(pl.multiple_of(n * SEG, SEG), SEG)
            kb = k_ref[rows, :]
            vt[:, rows] = v_ref[rows, :].astype(F32).T.astype(BF16)
            cb = c_ref[:, rows]
            x8 = []
            for hh in range(2):
                ch = jnp.sum(jnp.where(sub == 2 * pair + hh, cb, 0.0), axis=0, keepdims=True) * LOG2E
                x = jnp.zeros((8, LANES), F32)
                for e, t in enumerate(_bf16_terms(ch)):
                    x = jnp.where(sub == e, -t, x)
                x8.append(x)
            gap = jnp.zeros((HEAD_DIM - 8, LANES), F32)
            rt = jnp.concatenate([x8[1], gap, x8[0], gap], axis=0).T.astype(BF16)
            for hh in range(2):
                k2[hh, rows, :] = jnp.where(lane // HEAD_DIM == hh, kb, rt)
            return 0

        lax.fori_loop(0, s // SEG, stage, 0, unroll=2)

    q = q_ref[...]
    qlane = lax.broadcasted_iota(jnp.int32, (tq, LANES), 1)
    qp = []
    for hh in range(2):
        spare = HEAD_DIM * (1 - hh)
        ones = ((qlane >= spare) & (qlane < spare + C_TERMS)).astype(BF16)
        qp.append(jnp.where(qlane // HEAD_DIM == hh, q, ones))

    nfull = (i * tq) // tk
    krow = lax.broadcasted_iota(jnp.int32, (tk, tq), 0) + (nfull * tk - i * tq)
    visible = krow <= lax.broadcasted_iota(jnp.int32, (tk, tq), 1)

    def scores(j, buf):
        cols = pl.ds(pl.multiple_of(j * tk, tk), tk)
        for hh in range(2):
            buf[hh] = lax.dot_general(k2[hh, cols, :], qp[hh], _NT, preferred_element_type=F32)

    def absorb(j, buf, stats, masked):
        cols = pl.ds(pl.multiple_of(j * tk, tk), tk)
        out = []
        for hh in range(2):
            m, l, acc = stats[hh]
            st = jnp.where(visible, buf[hh], NEG) if masked else buf[hh]
            m_new = jnp.maximum(m, jnp.max(st, axis=0, keepdims=True))
            alpha = jnp.exp2(m - m_new)
            p = jnp.exp2(st - m_new)
            l = alpha * l + jnp.sum(p, axis=0, keepdims=True)
            vth = vt[hh * HEAD_DIM:(hh + 1) * HEAD_DIM, cols]
            acc = alpha * acc + jnp.dot(vth, p.astype(BF16), preferred_element_type=F32)
            out.append((m_new, l, acc))
        return tuple(out)

    def pair_step(t, stats):
        j = 2 * t
        scores(j + 1, sa)
        stats = absorb(j, sb, stats, False)
        scores(j + 2, sb)
        return absorb(j + 1, sa, stats, False)

    init = tuple((jnp.full((1, tq), NEG, F32), jnp.zeros((1, tq), F32), jnp.zeros((HEAD_DIM, tq), F32))
                 for _ in range(2))
    scores(nfull, sa)
    scores(0, sb)
    stats = absorb(nfull, sa, init, True)
    stats = lax.fori_loop(0, nfull // 2, pair_step, stats)
    stats = lax.cond(nfull % 2 == 1, lambda st: absorb(nfull - 1, sb, st, False), lambda st: st, stats)
    o_t = jnp.concatenate([acc / l for _, l, acc in stats], axis=0)
    o_ref[...] = o_t.T.astype(o_ref.dtype)


def _fox(qkv, c, *, batch, tq=512, tk=512):
    t, _ = qkv.shape
    s = t // batch
    nq = s // tq
    assert tk % tq == 0 and s % tk == 0
    pairs = WIDTH_B // LANES
    base = 3 * WIDTH_A // LANES
    kv = lambda off: pl.BlockSpec((s, LANES), lambda b, p, i: (b, base + off + p))
    return pl.pallas_call(
        functools.partial(_fox_kernel, tq=tq, tk=tk), grid=(batch, pairs, nq),
        in_specs=[pl.BlockSpec((N_HEADS_B, s), lambda b, p, i: (0, b)),
                  pl.BlockSpec((tq, LANES), lambda b, p, i: (b * nq + i, base + p)),
                  kv(pairs), kv(2 * pairs)],
        out_specs=pl.BlockSpec((tq, LANES), lambda b, p, i: (b * nq + i, p)),
        out_shape=jax.ShapeDtypeStruct((t, WIDTH_B), BF16),
        scratch_shapes=[pltpu.VMEM((2, s, LANES), BF16), pltpu.VMEM((LANES, s), BF16),
                        pltpu.VMEM((2, tk, tq), F32), pltpu.VMEM((2, tk, tq), F32)],
        compiler_params=_params("parallel", "parallel", "arbitrary"), name="fox",
    )(c, qkv, qkv, qkv)


def _retention_kernel(cdec_ref, q_ref, k_ref, v_ref, g_ref, din_ref, qdec_ref, kdec_ref, y_ref, state):
    head = pl.program_id(1)

    @pl.when(pl.program_id(2) == 0)
    def _():
        state[...] = jnp.zeros_like(state)

    chunk_dec = cdec_ref[head]
    c = RET_CHUNK
    for n in range(q_ref.shape[0] // c):
        rows = slice(n * c, (n + 1) * c)
        q = q_ref[rows, :]
        k = k_ref[rows, :]
        v = v_ref[rows, :]
        st = state[...]
        inner = lax.dot_general(q, k, _NT, preferred_element_type=F32) * din_ref[...]
        o = (jnp.dot(inner.astype(BF16), v, preferred_element_type=F32)
             + jnp.dot(q, st.astype(BF16), preferred_element_type=F32) * qdec_ref[...])
        kd = (k.astype(F32) * kdec_ref[...]).astype(BF16)
        state[...] = chunk_dec * st + lax.dot_general(kd, v, _TN, preferred_element_type=F32)
        mu = jnp.mean(o, axis=-1, keepdims=True)
        cen = o - mu
        var = jnp.mean(cen * cen, axis=-1, keepdims=True)
        gate = g_ref[rows, :].astype(F32)
        y_ref[rows, :] = (gate * jax.nn.sigmoid(gate) * (cen * lax.rsqrt(var + GN_EPS))).astype(y_ref.dtype)


def _retention(proj, *, batch, d_model, tb=1024):
    t, _ = proj.shape
    s = t // batch
    nb = s // tb
    dk = d_model // RET_HEADS
    dv = 2 * d_model // RET_HEADS
    c = RET_CHUNK
    log_g = jnp.log(1.0 - 2.0 ** (-5.0 - jnp.arange(RET_HEADS, dtype=F32)))
    idx = jnp.arange(c, dtype=F32)
    rel = idx[:, None] - idx[None, :]
    decay_in = jnp.where(rel >= 0, jnp.exp(log_g[:, None, None] * jnp.maximum(rel, 0.0)), 0.0)
    q_dec = jnp.exp(log_g[:, None] * (idx + 1.0))[:, :, None]
    k_dec = jnp.exp(log_g[:, None] * (c - 1.0 - idx))[:, :, None]
    chunk_dec = jnp.exp(log_g * c)
    per_head = lambda shape: pl.BlockSpec((None,) + shape, lambda b, h, i: (h, 0, 0))
    tok = lambda width, off: pl.BlockSpec((tb, width), lambda b, h, i: (b * nb + i, off + h))
    return pl.pallas_call(
        _retention_kernel, grid=(batch, RET_HEADS, nb),
        in_specs=[pl.BlockSpec(memory_space=pltpu.SMEM),
                  tok(dk, 0), tok(dk, d_model // dk), tok(dv, 2 * d_model // dv), tok(dv, 4 * d_model // dv),
                  per_head((c, c)), per_head((c, 1)), per_head((c, 1))],
        out_specs=tok(dv, 0),
        out_shape=jax.ShapeDtypeStruct((t, 2 * d_model), BF16),
        scratch_shapes=[pltpu.VMEM((dk, dv), F32)],
        compiler_params=_params("parallel", "parallel", "arbitrary"), name="retention",
    )(chunk_dec, proj, proj, proj, proj, decay_in, q_dec, k_dec)


def _post_mixer_kernel(*refs, n_in, th, final):
    x_ref = refs[0]
    ys = refs[1:1 + n_in]
    wo_ref, g_ref, w1_ref, w2_ref = refs[1 + n_in:5 + n_in]
    gf_ref = refs[5 + n_in] if final else None
    o_ref = refs[-1]
    y = jnp.concatenate([y_ref[...] for y_ref in ys], axis=1)
    x = x_ref[...] + jnp.dot(y, wo_ref[...], preferred_element_type=F32)
    h = _rms(x, g_ref[...]).astype(BF16)
    for c in range(w1_ref.shape[1] // th):
        sl = slice(c * th, (c + 1) * th)
        a = jnp.maximum(jnp.dot(h, w1_ref[:, sl], preferred_element_type=F32), 0.0)
        x = x + jnp.dot((a * a).astype(BF16), w2_ref[sl, :], preferred_element_type=F32)
    o_ref[...] = _rms(x, gf_ref[...]) if final else x


def _post_mixer(x2, ys, w_out, g, w1, w2, final_g=None, *, tm=512, th=512):
    t, d = x2.shape
    final = final_g is not None
    row = lambda width: pl.BlockSpec((tm, width), lambda i: (i, 0))
    in_specs = ([row(d)] + [row(y.shape[1]) for y in ys]
                + [_resident(w_out.shape), _resident((1, d)), _resident(w1.shape), _resident(w2.shape)])
    args = [x2, *ys, w_out, g.reshape(1, d), w1, w2]
    if final:
        in_specs.append(_resident((1, d)))
        args.append(final_g.reshape(1, d))
    return pl.pallas_call(
        functools.partial(_post_mixer_kernel, n_in=len(ys), th=th, final=final),
        grid=(t // tm,), in_specs=in_specs, out_specs=row(d),
        out_shape=jax.ShapeDtypeStruct((t, d), F32),
        compiler_params=_params("parallel"), name="post_mixer_final" if final else "post_mixer",
    )(*args)


def kernel(x, norm_mix_g, norm_ffn_g, even_w_in, even_b_f, even_w_out, odd_w_in, odd_w_out,
           ffn_w_in, ffn_w_out, final_g):
    batch, seq, d = x.shape
    depth = norm_mix_g.shape[0]
    x2 = x.reshape(batch * seq, d)
    slopes = 2.0 ** (-8.0 * (jnp.arange(N_HEADS_A, dtype=F32) + 1.0) / N_HEADS_A)
    n_qkv = 3 * WIDTH_A + 3 * WIDTH_B
    even_scale = jnp.ones((n_qkv,), F32).at[:WIDTH_A].set(HEAD_DIM ** -0.5 * LOG2E)
    even_scale = even_scale.at[3 * WIDTH_A:3 * WIDTH_A + WIDTH_B].set(HEAD_DIM ** -0.5 * LOG2E)
    odd_scale = jnp.ones((odd_w_in.shape[-1],), F32).at[d:2 * d].set((d // RET_HEADS) ** -0.5)
    for layer in range(depth):
        i = layer // 2
        if layer % 2 == 0:
            w = (even_w_in[i, :, :n_qkv] * even_scale).astype(BF16)
            wf_t = even_w_in[i, :, n_qkv:].T.astype(BF16)
            qkv, f_t = _norm_proj(x2, norm_mix_g[layer], w, wf_t)
            c = _gate_cumsum(f_t, even_b_f[i], batch=batch)
            oa = _dilated(qkv, slopes, batch=batch)
            ob = _fox(qkv, c, batch=batch)
            ys, w_out = [oa, ob], even_w_out[i].astype(BF16)
        else:
            w = (odd_w_in[i] * odd_scale).astype(BF16)
            proj = _norm_proj(x2, norm_mix_g[layer], w)
            ys, w_out = [_retention(proj, batch=batch, d_model=d)], odd_w_out[i].astype(BF16)
        x2 = _post_mixer(x2, ys, w_out, norm_ffn_g[layer], ffn_w_in[layer].astype(BF16),
                         ffn_w_out[layer].astype(BF16), final_g if layer == depth - 1 else None)
    return x2.reshape(batch, seq, d)
```

```python
import functools

import jax
import jax.numpy as jnp
from jax import lax
from jax.experimental import pallas as pl
from jax.experimental.pallas import tpu as pltpu

F32 = jnp.float32
BF16 = jnp.bfloat16

HEAD_DIM = 64
N_HEADS_A = 8
N_HEADS_B = 8
WIDTH_A = N_HEADS_A * HEAD_DIM
WIDTH_B = N_HEADS_B * HEAD_DIM
DILATIONS = (1, 4, 16)
SEG = 128
TILE_GROUP = 4
RET_HEADS = 4
RET_CHUNK = 128
RMS_EPS = 1e-6
GN_EPS = 1e-6
LANES = 128
NEG = -1e30
LOG2E = 1.4426950408889634
V7X_VMEM_LIMIT = 56 * 1024 * 1024

_NT = (((1,), (1,)), ((), ()))
_TN = (((0,), (0,)), ((), ()))


def _params(*sem):
    return pltpu.CompilerParams(dimension_semantics=sem, vmem_limit_bytes=V7X_VMEM_LIMIT)


def _resident(shape):
    nd = len(shape)
    return pl.BlockSpec(shape, lambda *_: (0,) * nd, pipeline_mode=pl.Buffered(1))


def _rms(x, g):
    ms = jnp.mean(x * x, axis=-1, keepdims=True)
    return x * lax.rsqrt(ms + RMS_EPS) * g


def _norm_proj_kernel(x_ref, g_ref, w_ref, *rest, tn, with_gate):
    if with_gate:
        wf_ref, o_ref, f_ref = rest
    else:
        (o_ref,) = rest
    h = _rms(x_ref[...], g_ref[...]).astype(BF16)
    for c in range(w_ref.shape[1] // tn):
        sl = slice(c * tn, (c + 1) * tn)
        o_ref[:, sl] = jnp.dot(h, w_ref[:, sl], preferred_element_type=F32).astype(o_ref.dtype)
    if with_gate:
        f_ref[...] = lax.dot_general(wf_ref[...], h, _NT, preferred_element_type=F32)


def _norm_proj(x2, g, w, wf_t=None, *, tm=512, tn=512):
    t, d = x2.shape
    n = w.shape[1]
    with_gate = wf_t is not None
    in_specs = [pl.BlockSpec((tm, d), lambda i: (i, 0)), _resident((1, d)), _resident((d, n))]
    out_specs = [pl.BlockSpec((tm, n), lambda i: (i, 0))]
    out_shape = [jax.ShapeDtypeStruct((t, n), BF16)]
    args = [x2, g.reshape(1, d), w]
    if with_gate:
        nh = wf_t.shape[0]
        in_specs.append(_resident((nh, d)))
        out_specs.append(pl.BlockSpec((nh, tm), lambda i: (0, i)))
        out_shape.append(jax.ShapeDtypeStruct((nh, t), F32))
        args.append(wf_t)
    res = pl.pallas_call(
        functools.partial(_norm_proj_kernel, tn=tn, with_gate=with_gate),
        grid=(t // tm,), in_specs=in_specs, out_specs=out_specs, out_shape=out_shape,
        compiler_params=_params("parallel"), name="norm_proj_gate" if with_gate else "norm_proj",
    )(*args)
    return res if with_gate else res[0]


def _gate_cumsum_kernel(f_ref, b_ref, c_ref):
    nh, s = f_ref.shape
    row = lax.broadcasted_iota(jnp.int32, (LANES, LANES), 0)
    col = lax.broadcasted_iota(jnp.int32, (LANES, LANES), 1)
    upper = (row <= col).astype(BF16)
    carry = jnp.zeros((nh, 1), F32)
    for n in range(s // LANES):
        sl = slice(n * LANES, (n + 1) * LANES)
        z = f_ref[:, sl] + b_ref[...]
        x = jnp.minimum(z, 0.0) - jnp.log1p(jnp.exp(-jnp.abs(z)))
        hi = x.astype(BF16)
        r1 = x - hi.astype(F32)
        mid = r1.astype(BF16)
        lo = (r1 - mid.astype(F32)).astype(BF16)
        c = (jnp.dot(hi, upper, preferred_element_type=F32)
             + jnp.dot(mid, upper, preferred_element_type=F32)
             + jnp.dot(lo, upper, preferred_element_type=F32)) + carry
        c_ref[:, sl] = c
        carry = c[:, LANES - 1:LANES]


def _gate_cumsum(f_t, b_f, *, batch):
    nh, t = f_t.shape
    s = t // batch
    return pl.pallas_call(
        _gate_cumsum_kernel, grid=(batch,),
        in_specs=[pl.BlockSpec((nh, s), lambda b: (0, b)), _resident((nh, 1))],
        out_specs=pl.BlockSpec((nh, s), lambda b: (0, b)),
        out_shape=jax.ShapeDtypeStruct((nh, t), F32),
        compiler_params=_params("parallel"), name="gate_cumsum",
    )(f_t, b_f.reshape(nh, 1).astype(F32))


def _dilated_kernel(slopes_ref, q_ref, k_ref, v_ref, o_ref,
                    qf, kf, vf, qg, kg, vg, qs, ks, vts, ob, lb, bias, sa, sb):
    pair = pl.program_id(1)
    s = q_ref.shape[0]
    nblk = s // SEG
    lane_head = lax.broadcasted_iota(jnp.int32, (SEG, LANES), 1) // HEAD_DIM
    ki = lax.broadcasted_iota(jnp.int32, (2 * SEG, SEG), 0)
    qi = lax.broadcasted_iota(jnp.int32, (2 * SEG, SEG), 1)
    steps = qi + SEG - ki
    band = (steps >= 0) & (steps <= SEG)
    stepsf = steps.astype(F32)

    qf[...] = q_ref[...].astype(F32)
    kf[...] = k_ref[...].astype(F32)
    vf[...] = v_ref[...].astype(F32)
    ks[0:SEG, :] = jnp.zeros((SEG, LANES), BF16)
    vts[:, 0:SEG] = jnp.zeros((LANES, SEG), BF16)

    f32_src = (qf, kf, vf)
    d_src = 1
    for g, d in enumerate(DILATIONS):
        per_stream = nblk // d
        ratio = d // d_src
        keep = g + 1 < len(DILATIONS) and d > 1
        f32_dst = (qg, kg, vg) if keep else None

        def stage(t, _, d=d, per_stream=per_stream, ratio=ratio, d_src=d_src, f32_src=f32_src, f32_dst=f32_dst):
            r = t // per_stream
            n = t - r * per_stream
            src = (r % d_src) * (s // d_src) + r // d_src + n * (SEG * ratio)
            dst = pl.multiple_of(t * SEG, SEG)
            rows = pl.ds(src, SEG, stride=ratio) if ratio > 1 else pl.ds(pl.multiple_of(src, SEG), SEG)
            q, k, v = (ref[rows, :] for ref in f32_src)
            if f32_dst is not None:
                for ref, val in zip(f32_dst, (q, k, v)):
                    ref[pl.ds(dst, SEG), :] = val
            qs[pl.ds(dst, SEG), :] = q.astype(BF16)
            ks[pl.ds(dst + SEG, SEG), :] = k.astype(BF16)
            vts[:, pl.ds(dst + SEG, SEG)] = v.T.astype(BF16)
            return 0

        lax.fori_loop(0, nblk, stage, 0, unroll=8)
        if keep:
            f32_src, d_src = f32_dst, d

        for hh in range(2):
            slope = slopes_ref[2 * pair + hh] * (d * LOG2E)
            b = jnp.where(band, -slope * stepsf, NEG)
            bias[2 * hh] = b
            bias[2 * hh + 1] = jnp.where(ki >= SEG, b, NEG)

        def scores(grp, buf):
            for a in range(TILE_GROUP):
                base = pl.multiple_of((grp * TILE_GROUP + a) * SEG, SEG)
                qt = qs[pl.ds(base, SEG), :]
                kk = ks[pl.ds(base, 2 * SEG), :]
                for hh in range(2):
                    qm = jnp.where(lane_head == hh, qt, jnp.zeros_like(qt))
                    buf[2 * a + hh] = lax.dot_general(kk, qm, _NT, preferred_element_type=F32)

        def absorb(grp, buf, d=d, per_stream=per_stream, g=g):
            for a in range(TILE_GROUP):
                t = grp * TILE_GROUP + a
                r = t // per_stream
                n = t - r * per_stream
                first = jnp.where(n == 0, 1, 0)
                base = pl.multiple_of(t * SEG, SEG)
                o_t, lses = [], []
                for hh in range(2):
                    st = buf[2 * a + hh] + bias[2 * hh + first]
                    m = jnp.max(st, axis=0, keepdims=True)
                    p = jnp.exp2(st - m)
                    den = jnp.sum(p, axis=0, keepdims=True)
                    vth = vts[hh * HEAD_DIM:(hh + 1) * HEAD_DIM, pl.ds(base, 2 * SEG)]
                    acc = jnp.dot(vth, p.astype(BF16), preferred_element_type=F32)
                    o_t.append(acc / den)
                    lses.append(m + jnp.log2(den))
                dst = n * (SEG * d) + r
                rows = pl.ds(dst, SEG, stride=d) if d > 1 else pl.ds(pl.multiple_of(dst, SEG), SEG)
                ob[g, rows, :] = jnp.concatenate(o_t, axis=0).T
                lb[g, rows, :] = jnp.concatenate(
                    [jnp.broadcast_to(l, (HEAD_DIM, SEG)) for l in lses], axis=0).T

        ngroups = nblk // TILE_GROUP

        def group_pair(u, _):
            scores(2 * u + 1, sb)
            absorb(2 * u, sa)
            scores(jnp.minimum(2 * u + 2, ngroups - 1), sa)
            absorb(2 * u + 1, sb)
            return 0

        scores(0, sa)
        lax.fori_loop(0, ngroups // 2, group_pair, 0)

    def mix(n, _):
        rows = pl.ds(pl.multiple_of(n * SEG, SEG), SEG)
        l0, l1, l2 = lb[0, rows, :], lb[1, rows, :], lb[2, rows, :]
        m = jnp.maximum(jnp.maximum(l0, l1), l2)
        w0, w1, w2 = jnp.exp2(l0 - m), jnp.exp2(l1 - m), jnp.exp2(l2 - m)
        num = w0 * ob[0, rows, :] + w1 * ob[1, rows, :] + w2 * ob[2, rows, :]
        o_ref[rows, :] = (num / (w0 + w1 + w2)).astype(o_ref.dtype)
        return 0

    lax.fori_loop(0, nblk, mix, 0, unroll=2)


def _dilated(qkv, slopes, *, batch):
    t, _ = qkv.shape
    s = t // batch
    assert s % (SEG * max(DILATIONS)) == 0
    pairs = WIDTH_A // LANES
    col = lambda off: pl.BlockSpec((s, LANES), lambda b, p: (b, off + p))
    return pl.pallas_call(
        _dilated_kernel, grid=(batch, pairs),
        in_specs=[pl.BlockSpec(memory_space=pltpu.SMEM), col(0), col(pairs), col(2 * pairs)],
        out_specs=pl.BlockSpec((s, LANES), lambda b, p: (b, p)),
        out_shape=jax.ShapeDtypeStruct((t, WIDTH_A), BF16),
        scratch_shapes=[pltpu.VMEM((s, LANES), F32)] * 6
        + [pltpu.VMEM((s, LANES), BF16), pltpu.VMEM((s + SEG, LANES), BF16), pltpu.VMEM((LANES, s + SEG), BF16)]
        + [pltpu.VMEM((len(DILATIONS), s, LANES), F32)] * 2
        + [pltpu.VMEM((4, 2 * SEG, SEG), F32)]
        + [pltpu.VMEM((2 * TILE_GROUP, 2 * SEG, SEG), F32)] * 2,
        compiler_params=_params("parallel", "parallel"), name="dilated",
    )(slopes, qkv, qkv, qkv)


C_TERMS = 3


def _bf16_terms(x):
    terms = []
    for _ in range(C_TERMS):
        t = x.astype(BF16).astype(F32)
        terms.append(t)
        x = x - t
    return terms


def _fox_kernel(c_ref, q_ref, k_ref, v_ref, o_ref, q2, k2, vt, sa, sb, m_s, l_s, acc_s, *, tb):
    pair = pl.program_id(1)
    u = pl.program_id(2)
    s = k_ref.shape[0]
    nq = s // tb
    lane = lax.broadcasted_iota(jnp.int32, (SEG, LANES), 1)

    @pl.when(u == 0)
    def _():
        sub = lax.broadcasted_iota(jnp.int32, (8, LANES), 0)

        def stage(n, _):
            rows = pl.ds(pl.multiple_of(n * SEG, SEG), SEG)
            kb = k_ref[rows, :]
            qb = q_ref[rows, :]
            vt[:, rows] = v_ref[rows, :].astype(F32).T.astype(BF16)
            cb = c_ref[:, rows]
            x8 = []
            for hh in range(2):
                ch = jnp.sum(jnp.where(sub == 2 * pair + hh, cb, 0.0), axis=0, keepdims=True) * LOG2E
                x = jnp.zeros((8, LANES), F32)
                for e, t in enumerate(_bf16_terms(ch)):
                    x = jnp.where(sub == e, -t, x)
                x8.append(x)
            gap = jnp.zeros((HEAD_DIM - 8, LANES), F32)
            rt = jnp.concatenate([x8[1], gap, x8[0], gap], axis=0).T.astype(BF16)
            for hh in range(2):
                spare = HEAD_DIM * (1 - hh)
                ones = ((lane >= spare) & (lane < spare + C_TERMS)).astype(BF16)
                k2[hh, rows, :] = jnp.where(lane // HEAD_DIM == hh, kb, rt)
                q2[hh, rows, :] = jnp.where(lane // HEAD_DIM == hh, qb, ones)
            return 0

        lax.fori_loop(0, s // SEG, stage, 0, unroll=2)

    qblk = (u, nq - 1 - u)
    causal = (lax.broadcasted_iota(jnp.int32, (tb, tb), 0) <= lax.broadcasted_iota(jnp.int32, (tb, tb), 1))
    bufs = (sa, sb)

    def slot_item(x):
        if x < 2:
            return x, qblk[x], qblk[x]
        f = x - 2
        in_first = f < u
        return jnp.where(in_first, 0, 1), jnp.where(in_first, qblk[0], qblk[1]), jnp.where(in_first, f, f - u)

    def scores(x):
        _, qb, kb = slot_item(x)
        qrows = pl.ds(pl.multiple_of(qb * tb, tb), tb)
        krows = pl.ds(pl.multiple_of(kb * tb, tb), tb)
        for hh in range(2):
            bufs[x % 2][hh] = lax.dot_general(k2[hh, krows, :], q2[hh, qrows, :], _NT,
                                              preferred_element_type=F32)

    def absorb(x):
        chain, _, kb = slot_item(x)
        krows = pl.ds(pl.multiple_of(kb * tb, tb), tb)
        for hh in range(2):
            idx = 2 * chain + hh
            st = bufs[x % 2][hh]
            if x < 2:
                st = jnp.where(causal, st, NEG)
                m_new = jnp.max(st, axis=0, keepdims=True)
                p = jnp.exp2(st - m_new)
                l = jnp.sum(p, axis=0, keepdims=True)
                acc = jnp.dot(vt[hh * HEAD_DIM:(hh + 1) * HEAD_DIM, krows], p.astype(BF16),
                              preferred_element_type=F32)
            else:
                m = m_s[idx, 0:1, :]
                m_new = jnp.maximum(m, jnp.max(st, axis=0, keepdims=True))
                alpha = jnp.exp2(m - m_new)
                p = jnp.exp2(st - m_new)
                l = alpha * l_s[idx, 0:1, :] + jnp.sum(p, axis=0, keepdims=True)
                acc = alpha * acc_s[idx] + jnp.dot(vt[hh * HEAD_DIM:(hh + 1) * HEAD_DIM, krows], p.astype(BF16),
                                                   preferred_element_type=F32)
            m_s[idx, 0:1, :] = m_new
            l_s[idx, 0:1, :] = l
            acc_s[idx] = acc

    nslots = nq + 1
    scores(0)
    for x in range(nslots):
        if x + 1 < nslots:
            scores(x + 1)
        absorb(x)
    for chain in range(2):
        o_t = jnp.concatenate([acc_s[2 * chain + hh] / l_s[2 * chain + hh, 0:1, :] for hh in range(2)], axis=0)
        o_ref[pl.ds(pl.multiple_of(qblk[chain] * tb, tb), tb), :] = o_t.T.astype(o_ref.dtype)


def _fox(qkv, c, *, batch, tb=512):
    t, _ = qkv.shape
    s = t // batch
    nq = s // tb
    assert nq % 2 == 0
    pairs = WIDTH_B // LANES
    base = 3 * WIDTH_A // LANES
    col = lambda off: pl.BlockSpec((s, LANES), lambda b, p, u: (b, base + off + p))
    return pl.pallas_call(
        functools.partial(_fox_kernel, tb=tb), grid=(batch, pairs, nq // 2),
        in_specs=[pl.BlockSpec((N_HEADS_B, s), lambda b, p, u: (0, b)), col(0), col(pairs), col(2 * pairs)],
        out_specs=pl.BlockSpec((s, LANES), lambda b, p, u: (b, p)),
        out_shape=jax.ShapeDtypeStruct((t, WIDTH_B), BF16),
        scratch_shapes=[pltpu.VMEM((2, s, LANES), BF16), pltpu.VMEM((2, s, LANES), BF16), pltpu.VMEM((LANES, s), BF16),
                        pltpu.VMEM((2, tb, tb), F32), pltpu.VMEM((2, tb, tb), F32),
                        pltpu.VMEM((4, 8, tb), F32), pltpu.VMEM((4, 8, tb), F32), pltpu.VMEM((4, HEAD_DIM, tb), F32)],
        compiler_params=_params("parallel", "parallel", "arbitrary"), name="fox",
    )(c, qkv, qkv, qkv)


def _retention_kernel(cdec_ref, q_ref, k_ref, v_ref, g_ref, din_ref, qdec_ref, kdec_ref, y_ref, state):
    head = pl.program_id(1)

    @pl.when(pl.program_id(2) == 0)
    def _():
        state[...] = jnp.zeros_like(state)

    chunk_dec = cdec_ref[head]
    c = RET_CHUNK
    for n in range(q_ref.shape[0] // c):
        rows = slice(n * c, (n + 1) * c)
        q = q_ref[rows, :]
        k = k_ref[rows, :]
        v = v_ref[rows, :]
        st = state[...]
        inner = lax.dot_general(q, k, _NT, preferred_element_type=F32) * din_ref[...]
        o = (jnp.dot(inner.astype(BF16), v, preferred_element_type=F32)
             + jnp.dot(q, st.astype(BF16), preferred_element_type=F32) * qdec_ref[...])
        kd = (k.astype(F32) * kdec_ref[...]).astype(BF16)
        state[...] = chunk_dec * st + lax.dot_general(kd, v, _TN, preferred_element_type=F32)
        mu = jnp.mean(o, axis=-1, keepdims=True)
        cen = o - mu
        var = jnp.mean(cen * cen, axis=-1, keepdims=True)
        gate = g_ref[rows, :].astype(F32)
        y_ref[rows, :] = (gate * jax.nn.sigmoid(gate) * (cen * lax.rsqrt(var + GN_EPS))).astype(y_ref.dtype)


def _retention(proj, *, batch, d_model, tb=1024):
    t, _ = proj.shape
    s = t // batch
    nb = s // tb
    dk = d_model // RET_HEADS
    dv = 2 * d_model // RET_HEADS
    c = RET_CHUNK
    log_g = jnp.log(1.0 - 2.0 ** (-5.0 - jnp.arange(RET_HEADS, dtype=F32)))
    idx = jnp.arange(c, dtype=F32)
    rel = idx[:, None] - idx[None, :]
    decay_in = jnp.where(rel >= 0, jnp.exp(log_g[:, None, None] * jnp.maximum(rel, 0.0)), 0.0)
    q_dec = jnp.exp(log_g[:, None] * (idx + 1.0))[:, :, None]
    k_dec = jnp.exp(log_g[:, None] * (c - 1.0 - idx))[:, :, None]
    chunk_dec = jnp.exp(log_g * c)
    per_head = lambda shape: pl.BlockSpec((None,) + shape, lambda b, h, i: (h, 0, 0))
    tok = lambda width, off: pl.BlockSpec((tb, width), lambda b, h, i: (b * nb + i, off + h))
    return pl.pallas_call(
        _retention_kernel, grid=(batch, RET_HEADS, nb),
        in_specs=[pl.BlockSpec(memory_space=pltpu.SMEM),
                  tok(dk, 0), tok(dk, d_model // dk), tok(dv, 2 * d_model // dv), tok(dv, 4 * d_model // dv),
                  per_head((c, c)), per_head((c, 1)), per_head((c, 1))],
        out_specs=tok(dv, 0),
        out_shape=jax.ShapeDtypeStruct((t, 2 * d_model), BF16),
        scratch_shapes=[pltpu.VMEM((dk, dv), F32)],
        compiler_params=_params("parallel", "parallel", "arbitrary"), name="retention",
    )(chunk_dec, proj, proj, proj, proj, decay_in, q_dec, k_dec)


def _post_mixer_kernel(*refs, n_in, th, final):
    x_ref = refs[0]
    ys = refs[1:1 + n_in]
    wo_ref, g_ref, w1_ref, w2_ref = refs[1 + n_in:5 + n_in]
    gf_ref = refs[5 + n_in] if final else None
    o_ref = refs[-1]
    y = jnp.concatenate([y_ref[...] for y_ref in ys], axis=1)
    x = x_ref[...] + jnp.dot(y, wo_ref[...], preferred_element_type=F32)
    h = _rms(x, g_ref[...]).astype(BF16)
    for c in range(w1_ref.shape[1] // th):
        sl = slice(c * th, (c + 1) * th)
        a = jnp.maximum(jnp.dot(h, w1_ref[:, sl], preferred_element_type=F32), 0.0)
        x = x + jnp.dot((a * a).astype(BF16), w2_ref[sl, :], preferred_element_type=F32)
    o_ref[...] = _rms(x, gf_ref[...]) if final else x


def _post_mixer(x2, ys, w_out, g, w1, w2, final_g=None, *, tm=512, th=512):
    t, d = x2.shape
    final = final_g is not None
    row = lambda width: pl.BlockSpec((tm, width), lambda i: (i, 0))
    in_specs = ([row(d)] + [row(y.shape[1]) for y in ys]
                + [_resident(w_out.shape), _resident((1, d)), _resident(w1.shape), _resident(w2.shape)])
    args = [x2, *ys, w_out, g.reshape(1, d), w1, w2]
    if final:
        in_specs.append(_resident((1, d)))
        args.append(final_g.reshape(1, d))
    return pl.pallas_call(
        functools.partial(_post_mixer_kernel, n_in=len(ys), th=th, final=final),
        grid=(t // tm,), in_specs=in_specs, out_specs=row(d),
        out_shape=jax.ShapeDtypeStruct((t, d), F32),
        compiler_params=_params("parallel"), name="post_mixer_final" if final else "post_mixer",
    )(*args)


def kernel(x, norm_mix_g, norm_ffn_g, even_w_in, even_b_f, even_w_out, odd_w_in, odd_w_out,
           ffn_w_in, ffn_w_out, final_g):
    batch, seq, d = x.shape
    depth = norm_mix_g.shape[0]
    x2 = x.reshape(batch * seq, d)
    slopes = 2.0 ** (-8.0 * (jnp.arange(N_HEADS_A, dtype=F32) + 1.0) / N_HEADS_A)
    n_qkv = 3 * WIDTH_A + 3 * WIDTH_B
    even_scale = jnp.ones((n_qkv,), F32).at[:WIDTH_A].set(HEAD_DIM ** -0.5 * LOG2E)
    even_scale = even_scale.at[3 * WIDTH_A:3 * WIDTH_A + WIDTH_B].set(HEAD_DIM ** -0.5 * LOG2E)
    odd_scale = jnp.ones((odd_w_in.shape[-1],), F32).at[d:2 * d].set((d // RET_HEADS) ** -0.5)
    for layer in range(depth):
        i = layer // 2
        if layer % 2 == 0:
            w = (even_w_in[i, :, :n_qkv] * even_scale).astype(BF16)
            wf_t = even_w_in[i, :, n_qkv:].T.astype(BF16)
            qkv, f_t = _norm_proj(x2, norm_mix_g[layer], w, wf_t)
            c = _gate_cumsum(f_t, even_b_f[i], batch=batch)
            oa = _dilated(qkv, slopes, batch=batch)
            ob = _fox(qkv, c, batch=batch)
            ys, w_out = [oa, ob], even_w_out[i].astype(BF16)
        else:
            w = (odd_w_in[i] * odd_scale).astype(BF16)
            proj = _norm_proj(x2, norm_mix_g[layer], w)
            ys, w_out = [_retention(proj, batch=batch, d_model=d)], odd_w_out[i].astype(BF16)
        x2 = _post_mixer(x2, ys, w_out, norm_ffn_g[layer], ffn_w_in[layer].astype(BF16),
                         ffn_w_out[layer].astype(BF16), final_g if layer == depth - 1 else None)
    return x2.reshape(batch, seq, d)
```
